```python
import math
import jax, jax.numpy as jnp
from jax import lax
import numpy as np

D_MODEL = 1024
BATCH = 8
SEQ = 4096
DEPTH = 2

N_A_LAYERS = DEPTH // 2
N_B_LAYERS = DEPTH - N_A_LAYERS
CONV_WIDTH = 3
N_HEADS = 8
HEAD_DIM = D_MODEL // (2 * N_HEADS)
V_HEAD_DIM = 2 * HEAD_DIM
QK_WIDTH = N_HEADS * 2 * HEAD_DIM
V_WIDTH = N_HEADS * V_HEAD_DIM
D_FF = ((8 * D_MODEL // 3 + 127) // 128) * 128
PLE_DIM = 256
N_BUCKETS = 32
MAX_DISTANCE = 128
Q_BLOCK = 128
EPS = 1e-6

kernel_name = "yoco_shortconv_diffattn_convffn_block"


def rms_norm(x, g):
    xf = x.astype(jnp.float32)
    y = xf * lax.rsqrt(jnp.mean(xf * xf, axis=-1, keepdims=True) + EPS)
    return (y * g.astype(jnp.float32)).astype(x.dtype)


def causal_dwconv(x, w):
    c = x.shape[-1]
    return lax.conv_general_dilated(
        x, w[:, None, :].astype(x.dtype), window_strides=(1,),
        padding=[(CONV_WIDTH - 1, 0)], dimension_numbers=('NWC', 'WIO', 'NWC'),
        feature_group_count=c)


def rel_bucket(dist):
    max_exact = N_BUCKETS // 2
    large = max_exact + (jnp.log(jnp.maximum(dist, 1).astype(jnp.float32) / max_exact)
                         / math.log(MAX_DISTANCE / max_exact)
                         * (N_BUCKETS - max_exact)).astype(jnp.int32)
    large = jnp.minimum(large, N_BUCKETS - 1)
    return jnp.where(dist < max_exact, dist, large)


def short_conv_mixer(h, w_in, w_conv, w_out):
    gb, gc, u = jnp.split(h @ w_in, 3, axis=-1)
    return (gb * causal_dwconv(gc * u, w_conv)) @ w_out


def conv_ffn(h, w_up, w_conv, w_down):
    gate, up = jnp.split(causal_dwconv(h @ w_up, w_conv), 2, axis=-1)
    return (jax.nn.silu(gate) * up) @ w_down


def per_layer_embed(x, p_i, g_pre, w_gate, w_proj, g_post):
    gate = jax.nn.sigmoid(rms_norm(x, g_pre) @ w_gate)
    return rms_norm(gate * (p_i @ w_proj), g_post)


def shared_kv(x, g_kv, w_kv):
    b, s, _ = x.shape
    k, v = jnp.split(rms_norm(x, g_kv) @ w_kv, [QK_WIDTH], axis=-1)
    return (k.reshape(b, s, N_HEADS, 2, HEAD_DIM),
            v.reshape(b, s, N_HEADS, V_HEAD_DIM))


def diff_attention(h, k, v, dist_bias, w_q, lam_vecs, subln_g, w_o, lambda_init):
    b, s, _ = h.shape
    nb = s // Q_BLOCK
    q = (h @ w_q).reshape(b, nb, Q_BLOCK, N_HEADS, 2, HEAD_DIM).transpose(1, 0, 2, 3, 4, 5)
    lv = lam_vecs.astype(jnp.float32)
    lam = jnp.exp(jnp.sum(lv[0] * lv[1])) - jnp.exp(jnp.sum(lv[2] * lv[3])) + lambda_init
    kf = k.astype(jnp.float32)
    vf = v.astype(jnp.float32)
    bias_tab = dist_bias.astype(jnp.float32)
    k_pos = jnp.arange(s)
    scale = HEAD_DIM ** -0.5

    def block(args):
        qb, blk = args
        q_pos = blk * Q_BLOCK + jnp.arange(Q_BLOCK)
        dist = q_pos[:, None] - k_pos[None, :]
        bias = jnp.take(bias_tab, jnp.maximum(dist, 0), axis=0).transpose(2, 0, 1)
        logits = jnp.einsum('bqhmd,bkhmd->bhmqk', qb.astype(jnp.float32), kf) * scale
        logits = logits + bias[None, :, None]
        logits = jnp.where((dist >= 0)[None, None, None], logits, -jnp.inf)
        probs = jax.nn.softmax(logits, axis=-1)
        attn = probs[:, :, 0] - lam * probs[:, :, 1]
        return jnp.einsum('bhqk,bkhe->bqhe', attn, vf)

    o = lax.map(block, (q, jnp.arange(nb)))
    o = o.transpose(1, 0, 2, 3, 4).reshape(b, s, N_HEADS, V_HEAD_DIM)
    o = rms_norm(o, subln_g) * (1.0 - lambda_init)
    return o.reshape(b, s, V_WIDTH).astype(h.dtype) @ w_o


def setup_inputs(seed: int = 0) -> dict:
    key = jax.random.key(seed)
    ks = jax.random.split(key, 24)
    f32 = jnp.float32

    def nrm(k, shape, scale):
        return jax.random.normal(k, shape, f32) * scale

    def gain(k, shape):
        return 1.0 + 0.05 * jax.random.normal(k, shape, f32)

    return {
        "x": nrm(ks[0], (BATCH, SEQ, D_MODEL), 1.0),
        "p": nrm(ks[1], (DEPTH, BATCH, SEQ, PLE_DIM), 1.0),
        "g_pre_mix": gain(ks[2], (DEPTH, D_MODEL)),
        "g_post_mix": gain(ks[3], (DEPTH, D_MODEL)),
        "g_pre_ffn": gain(ks[4], (DEPTH, D_MODEL)),
        "g_post_ffn": gain(ks[5], (DEPTH, D_MODEL)),
        "g_pre_ple": gain(ks[6], (DEPTH, D_MODEL)),
        "g_post_ple": gain(ks[7], (DEPTH, D_MODEL)),
        "w_sc_in": nrm(ks[8], (N_A_LAYERS, D_MODEL, 3 * D_MODEL), D_MODEL ** -0.5),
        "w_sc_conv": nrm(ks[9], (N_A_LAYERS, CONV_WIDTH, D_MODEL), CONV_WIDTH ** -0.5),
        "w_sc_out": nrm(ks[10], (N_A_LAYERS, D_MODEL, D_MODEL), D_MODEL ** -0.5),
        "g_kv": gain(ks[11], (D_MODEL,)),
        "w_kv": nrm(ks[12], (D_MODEL, QK_WIDTH + V_WIDTH), D_MODEL ** -0.5),
        "rel_bias": nrm(ks[13], (N_BUCKETS, N_HEADS), 0.5),
        "w_q": nrm(ks[14], (N_B_LAYERS, D_MODEL, QK_WIDTH), D_MODEL ** -0.5),
        "diff_lambda": nrm(ks[15], (N_B_LAYERS, 4, HEAD_DIM), 0.1),
        "g_subln": gain(ks[16], (N_B_LAYERS, V_HEAD_DIM)),
        "w_o": nrm(ks[17], (N_B_LAYERS, V_WIDTH, D_MODEL), V_WIDTH ** -0.5),
        "w_ffn_up": nrm(ks[18], (DEPTH, D_MODEL, 2 * D_FF), D_MODEL ** -0.5),
        "w_ffn_conv": nrm(ks[19], (DEPTH, CONV_WIDTH, 2 * D_FF), CONV_WIDTH ** -0.5),
        "w_ffn_down": nrm(ks[20], (DEPTH, D_FF, D_MODEL), D_FF ** -0.5),
        "w_ple_gate": nrm(ks[21], (DEPTH, D_MODEL, D_MODEL), D_MODEL ** -0.5),
        "w_ple_proj": nrm(ks[22], (DEPTH, PLE_DIM, D_MODEL), PLE_DIM ** -0.5),
    }


def reference(x, p, g_pre_mix, g_post_mix, g_pre_ffn, g_post_ffn, g_pre_ple, g_post_ple,
              w_sc_in, w_sc_conv, w_sc_out, g_kv, w_kv, rel_bias, w_q, diff_lambda,
              g_subln, w_o, w_ffn_up, w_ffn_conv, w_ffn_down, w_ple_gate, w_ple_proj):
    s = x.shape[1]
    dist_bias = jnp.take(rel_bias, rel_bucket(jnp.arange(s)), axis=0)
    k = None
    v = None
    for i in range(DEPTH):
        h = rms_norm(x, g_pre_mix[i])
        if i < N_A_LAYERS:
            mix = short_conv_mixer(h, w_sc_in[i], w_sc_conv[i], w_sc_out[i])
        else:
            j = i - N_A_LAYERS
            if j == 0:
                k, v = shared_kv(x, g_kv, w_kv)
            lambda_init = 0.8 - 0.6 * math.exp(-0.3 * i)
            mix = diff_attention(h, k, v, dist_bias, w_q[j], diff_lambda[j], g_subln[j],
                                 w_o[j], lambda_init)
        x = x + rms_norm(mix, g_post_mix[i])
        ffn = conv_ffn(rms_norm(x, g_pre_ffn[i]), w_ffn_up[i], w_ffn_conv[i], w_ffn_down[i])
        x = x + rms_norm(ffn, g_post_ffn[i])
        x = x + per_layer_embed(x, p[i], g_pre_ple[i], w_ple_gate[i], w_ple_proj[i], g_post_ple[i])
    return x
```

```python
import functools
import math

import numpy as np
import jax
import jax.numpy as jnp
from jax import lax
from jax.experimental import pallas as pl
from jax.experimental.pallas import tpu as pltpu

F32 = jnp.float32
BF16 = jnp.bfloat16

EPS = 1e-6
CONV_WIDTH = 3
N_HEADS = 8
HEAD_DIM = 64
V_HEAD_DIM = 2 * HEAD_DIM
N_BUCKETS = 32
MAX_DISTANCE = 128

V7X_SUBLANES = 8
V7X_MXU_WIDTH = 256
V7X_VMEM_LIMIT = 56 * 1024 * 1024

TOKEN_TILE = 256
ATTN_TILE = 512
HALO = V7X_SUBLANES


def _dot(a, b):
    return jnp.dot(a, b, preferred_element_type=F32)


def _dot_nt(a, b):
    return lax.dot_general(a, b, (((1,), (1,)), ((), ())), preferred_element_type=F32)


def _normalize(x):
    return x * lax.rsqrt(jnp.mean(x * x, axis=-1, keepdims=True) + EPS)


def _rms(x, g):
    return _normalize(x) * g


def _sigmoid(x):
    return 1.0 / (1.0 + jnp.exp(-x))


def _causal_conv_cols(buf, y, wconv_ref, cols, tm):
    buf[HALO:HALO + tm, cols] = y
    return (wconv_ref[0:1, cols] * buf[HALO - 2:HALO - 2 + tm, cols]
            + wconv_ref[1:2, cols] * buf[HALO - 1:HALO - 1 + tm, cols]
            + wconv_ref[2:3, cols] * y)


def _resident(shape):
    return pl.BlockSpec(shape, lambda *_: (0,) * len(shape), pipeline_mode=pl.Buffered(1))


def _mixer_kernel(x_ref, gpre_ref, gpost_ref, win_ref, wconv_ref, wout_ref, o_ref,
                  zbuf, mbuf, *, tm, tiles_per_seq, d):
    i = pl.program_id(0)

    @pl.when(lax.rem(i, tiles_per_seq) == 0)
    def _():
        zbuf[0:HALO, :] = jnp.zeros((HALO, d), F32)

    x = x_ref[...]
    h = _rms(x, gpre_ref[...]).astype(BF16)
    cw = V7X_MXU_WIDTH
    for c in range(d // cw):
        cols = slice(c * cw, (c + 1) * cw)
        gb = _dot(h, win_ref[:, c * cw:(c + 1) * cw])
        gc = _dot(h, win_ref[:, d + c * cw:d + (c + 1) * cw])
        u = _dot(h, win_ref[:, 2 * d + c * cw:2 * d + (c + 1) * cw])
        conv = _causal_conv_cols(zbuf, gc * u, wconv_ref, cols, tm)
        mbuf[:, cols] = (gb * conv).astype(BF16)
    zbuf[0:HALO, :] = zbuf[tm:tm + HALO, :]
    mix = _dot(mbuf[...], wout_ref[...])
    o_ref[...] = x + _rms(mix, gpost_ref[...])


def _mixer(x2, g_pre, g_post, w_in, w_conv, w_out, seq):
    t, d = x2.shape
    tm = TOKEN_TILE
    kern = functools.partial(_mixer_kernel, tm=tm, tiles_per_seq=seq // tm, d=d)
    row = pl.BlockSpec((tm, d), lambda i: (i, 0))
    return pl.pallas_call(
        kern,
        grid=(t // tm,),
        in_specs=[row, _resident((1, d)), _resident((1, d)), _resident((d, 3 * d)),
                  _resident((CONV_WIDTH, d)), _resident((d, d))],
        out_specs=row,
        out_shape=jax.ShapeDtypeStruct((t, d), F32),
        scratch_shapes=[pltpu.VMEM((tm + HALO, d), F32), pltpu.VMEM((tm, d), BF16)],
        compiler_params=pltpu.CompilerParams(
            dimension_semantics=("arbitrary",), vmem_limit_bytes=V7X_VMEM_LIMIT),
        name="sconv_mixer",
    )(x2, g_pre, g_post, w_in, w_conv, w_out)


def _ffn_ple_kernel(*refs, tm, tiles_per_seq, d, f, with_attn_out):
    if with_attn_out:
        (x_ref, p_ref, a_ref, gains_ref, wo_ref, wup_ref, wconv_ref, wdown_ref, wgate_ref,
         wproj_ref, o_ref, ybuf, abuf) = refs
    else:
        (x_ref, p_ref, gains_ref, wup_ref, wconv_ref, wdown_ref, wgate_ref,
         wproj_ref, o_ref, ybuf, abuf) = refs
    i = pl.program_id(0)

    @pl.when(lax.rem(i, tiles_per_seq) == 0)
    def _():
        ybuf[0:HALO, :] = jnp.zeros((HALO, 2 * f), F32)

    x = x_ref[...]
    if with_attn_out:
        x = x + _rms(_dot(a_ref[...], wo_ref[...]), gains_ref[4:5, :])

    hn = _rms(x, gains_ref[0:1, :]).astype(BF16)
    cw = V7X_MXU_WIDTH
    for c in range(f // cw):
        gcols = slice(c * cw, (c + 1) * cw)
        ucols = slice(f + c * cw, f + (c + 1) * cw)
        gate = _causal_conv_cols(ybuf, _dot(hn, wup_ref[:, c * cw:(c + 1) * cw]), wconv_ref, gcols, tm)
        up = _causal_conv_cols(ybuf, _dot(hn, wup_ref[:, f + c * cw:f + (c + 1) * cw]), wconv_ref, ucols, tm)
        abuf[:, gcols] = (gate * _sigmoid(gate) * up).astype(BF16)
    ybuf[0:HALO, :] = ybuf[tm:tm + HALO, :]
    x = x + _rms(_dot(abuf[...], wdown_ref[...]), gains_ref[1:2, :])

    gate = _sigmoid(_dot(_rms(x, gains_ref[2:3, :]).astype(BF16), wgate_ref[...]))
    emb = _dot(p_ref[...].astype(BF16), wproj_ref[...])
    o_ref[...] = x + _rms(gate * emb, gains_ref[3:4, :])


def _ffn_ple(x2, p2, gains, w_up, w_conv, w_down, w_gate, w_proj, seq, attn=None, w_o=None):
    t, d = x2.shape
    f = w_down.shape[0]
    pd = p2.shape[1]
    tm = TOKEN_TILE
    with_attn_out = attn is not None
    kern = functools.partial(_ffn_ple_kernel, tm=tm, tiles_per_seq=seq // tm, d=d, f=f,
                             with_attn_out=with_attn_out)
    row = pl.BlockSpec((tm, d), lambda i: (i, 0))
    prow = pl.BlockSpec((tm, pd), lambda i: (i, 0))
    args = [x2, p2]
    specs = [row, prow]
    if with_attn_out:
        args.append(attn)
        specs.append(row)
    args.append(gains)
    specs.append(_resident(gains.shape))
    if with_attn_out:
        args.append(w_o)
        specs.append(_resident((d, d)))
    args += [w_up, w_conv, w_down, w_gate, w_proj]
    specs += [_resident((d, 2 * f)), _resident((CONV_WIDTH, 2 * f)), _resident((f, d)),
              _resident((d, d)), _resident((pd, d))]
    return pl.pallas_call(
        kern,
        grid=(t // tm,),
        in_specs=specs,
        out_specs=row,
        out_shape=jax.ShapeDtypeStruct((t, d), F32),
        scratch_shapes=[pltpu.VMEM((tm + HALO, 2 * f), F32), pltpu.VMEM((tm, f), BF16)],
        compiler_params=pltpu.CompilerParams(
            dimension_semantics=("arbitrary",), vmem_limit_bytes=V7X_VMEM_LIMIT),
        name="attn_out_ffn_ple" if with_attn_out else "ffn_ple",
    )(*args)


def _qkv_kernel(x_ref, gq_ref, gkv_ref, wqt_ref, wk_ref, wvt_ref, qt_ref, k_ref, vt_ref):
    xn = _normalize(x_ref[0])
    h = (xn * gq_ref[...]).astype(BF16)
    hk = (xn * gkv_ref[...]).astype(BF16)
    qt_ref[0] = (_dot_nt(wqt_ref[...], h) * (HEAD_DIM ** -0.5)).astype(BF16)
    k_ref[0] = _dot(hk, wk_ref[...]).astype(BF16)
    vt_ref[0] = _dot_nt(wvt_ref[...], hk).astype(BF16)


def _qkv(x3, g_q, g_kv, wq_t, w_k, wv_t):
    b, s, d = x3.shape
    tm = TOKEN_TILE
    row = pl.BlockSpec((1, tm, d), lambda bi, i: (bi, i, 0))
    col = pl.BlockSpec((1, d, tm), lambda bi, i: (bi, 0, i))
    return pl.pallas_call(
        _qkv_kernel,
        grid=(b, s // tm),
        in_specs=[row, _resident((1, d)), _resident((1, d)), _resident((d, d)),
                  _resident((d, d)), _resident((d, d))],
        out_specs=[col, row, col],
        out_shape=[jax.ShapeDtypeStruct((b, d, s), BF16), jax.ShapeDtypeStruct((b, s, d), BF16),
                   jax.ShapeDtypeStruct((b, d, s), BF16)],
        compiler_params=pltpu.CompilerParams(
            dimension_semantics=("arbitrary", "arbitrary"), vmem_limit_bytes=V7X_VMEM_LIMIT),
        name="qkv_proj",
    )(x3, g_q, g_kv, wq_t, w_k, wv_t)


def _bucket_thresholds(seq):
    dist = np.arange(seq)
    max_exact = N_BUCKETS // 2
    large = max_exact + (np.log(np.maximum(dist, 1).astype(np.float32) / np.float32(max_exact))
                         / np.float32(math.log(MAX_DISTANCE / max_exact))
                         * np.float32(N_BUCKETS - max_exact)).astype(np.int32)
    bucket = np.where(dist < max_exact, dist, np.minimum(large, N_BUCKETS - 1))
    assert np.all(np.diff(bucket) >= 0)
    thr = [int(np.argmax(bucket >= b)) if np.any(bucket >= b) else seq for b in range(N_BUCKETS)]
    assert np.all(bucket[MAX_DISTANCE:] == N_BUCKETS - 1)
    return thr


def _attn_kernel(rb_ref, qt_ref, k_ref, vt_ref, lam_ref, g_ref, o_ref,
                 bias_sc, q_sc, m_sc, l_sc, acc_sc, *, t, thresholds, lambda_init):
    h = pl.program_id(0)
    bi = pl.program_id(1)
    qi = pl.program_id(2)

    @pl.when((bi == 0) & (qi == 0))
    def _():
        row = lax.broadcasted_iota(jnp.int32, (t, t), 0)
        col = lax.broadcasted_iota(jnp.int32, (t, t), 1)
        for tile, delta in ((0, 0), (1, t)):
            dist = col - row + delta
            val = jnp.full((t, t), rb_ref[0, h], F32)
            for b in range(1, N_BUCKETS):
                val = jnp.where(dist >= thresholds[b], rb_ref[b, h], val)
            if tile == 0:
                val = jnp.where(dist >= 0, val, -jnp.inf)
            bias_sc[tile] = val

    zero = jnp.zeros((HEAD_DIM, t), BF16)
    q_sc[0, 0:HEAD_DIM, :] = qt_ref[0, 0:HEAD_DIM, :]
    q_sc[0, HEAD_DIM:, :] = zero
    q_sc[1, 0:HEAD_DIM, :] = zero
    q_sc[1, HEAD_DIM:, :] = qt_ref[0, HEAD_DIM:, :]

    def tile_step(kstart, bias_tile, shift, first):
        kblk = k_ref[0, pl.ds(kstart, t), :]
        vtblk = vt_ref[0, :, pl.ds(kstart, t)]
        for mp in range(2):
            s = _dot(kblk, q_sc[mp])
            if bias_tile is not None:
                s = s + bias_sc[bias_tile]
            mt = jnp.max(s, axis=0, keepdims=True) + shift
            if first:
                m_new = mt
            else:
                m_old = m_sc[mp]
                m_new = jnp.maximum(m_old, mt)
                alpha = jnp.exp(m_old - m_new)
            p = jnp.exp(s - (m_new - shift))
            psum = jnp.sum(p, axis=0, keepdims=True)
            pv = _dot(vtblk, p.astype(BF16))
            if first:
                l_sc[mp] = psum
                acc_sc[mp] = pv
            else:
                l_sc[mp] = alpha * l_sc[mp] + psum
                acc_sc[mp] = alpha * acc_sc[mp] + pv
            m_sc[mp] = m_new

    q0 = pl.multiple_of(qi * t, t)
    tile_step(q0, 0, 0.0, True)

    @pl.when(qi >= 1)
    def _():
        tile_step(pl.multiple_of((qi - 1) * t, t), 1, 0.0, False)

    far_bias = rb_ref[N_BUCKETS - 1, h]

    def far_body(j, carry):
        tile_step(pl.multiple_of(j * t, t), None, far_bias, False)
        return carry

    lax.fori_loop(0, jnp.maximum(qi - 1, 0), far_body, 0)

    lv = lam_ref[...]
    lam = (jnp.exp(jnp.sum(lv[0:1] * lv[1:2], axis=-1, keepdims=True))
           - jnp.exp(jnp.sum(lv[2:3] * lv[3:4], axis=-1, keepdims=True)) + lambda_init)
    o = acc_sc[0] * (1.0 / l_sc[0]) - acc_sc[1] * (lam / l_sc[1])
    o = o * lax.rsqrt(jnp.mean(o * o, axis=0, keepdims=True) + EPS) * g_ref[...]
    o = o * (1.0 - lambda_init)
    o_ref[0] = o.T.astype(BF16)


def _attention(qt, k, vt, rel_bias, lam_vecs, g_subln, lambda_init):
    b, s, _ = k.shape
    t = ATTN_TILE
    assert s % t == 0 and t >= MAX_DISTANCE
    hd = V_HEAD_DIM
    kern = functools.partial(_attn_kernel, t=t, thresholds=_bucket_thresholds(s),
                             lambda_init=lambda_init)
    return pl.pallas_call(
        kern,
        grid=(N_HEADS, b, s // t),
        in_specs=[
            pl.BlockSpec(memory_space=pltpu.SMEM),
            pl.BlockSpec((1, hd, t), lambda h, bi, qi: (bi, h, qi)),
            pl.BlockSpec((1, s, hd), lambda h, bi, qi: (bi, 0, h)),
            pl.BlockSpec((1, hd, s), lambda h, bi, qi: (bi, h, 0)),
            _resident(lam_vecs.shape),
            _resident((hd, 1)),
        ],
        out_specs=pl.BlockSpec((1, t, hd), lambda h, bi, qi: (bi, qi, h)),
        out_shape=jax.ShapeDtypeStruct((b, s, N_HEADS * hd), BF16),
        scratch_shapes=[
            pltpu.VMEM((2, t, t), F32),
            pltpu.VMEM((2, hd, t), BF16),
            pltpu.VMEM((2, 1, t), F32),
            pltpu.VMEM((2, 1, t), F32),
            pltpu.VMEM((2, hd, t), F32),
        ],
        compiler_params=pltpu.CompilerParams(
            dimension_semantics=("arbitrary", "arbitrary", "arbitrary"),
            vmem_limit_bytes=V7X_VMEM_LIMIT),
        name="diff_attention",
    )(rel_bias, qt, k, vt, lam_vecs, g_subln)


def kernel(x, p, g_pre_mix, g_post_mix, g_pre_ffn, g_post_ffn, g_pre_ple, g_post_ple,
           w_sc_in, w_sc_conv, w_sc_out, g_kv, w_kv, rel_bias, w_q, diff_lambda,
           g_subln, w_o, w_ffn_up, w_ffn_conv, w_ffn_down, w_ple_gate, w_ple_proj):
    b, s, d = x.shape
    depth = p.shape[0]
    n_a = depth // 2
    qk_width = N_HEADS * 2 * HEAD_DIM
    x2 = x.reshape(b * s, d)
    p2 = p.reshape(depth, b * s, p.shape[-1])
    bf = lambda w: w.astype(BF16)

    for i in range(depth):
        attn = None
        if i < n_a:
            x2 = _mixer(x2, g_pre_mix[i][None], g_post_mix[i][None], bf(w_sc_in[i]),
                        w_sc_conv[i], bf(w_sc_out[i]), s)
        else:
            j = i - n_a
            assert j == 0, "one attention layer: K/V and Q read the same stream"
            w_k, w_v = w_kv[:, :qk_width], w_kv[:, qk_width:]
            lambda_init = 0.8 - 0.6 * math.exp(-0.3 * i)
            qt, k, vt = _qkv(x2.reshape(b, s, d), g_pre_mix[i][None], g_kv[None],
                             bf(w_q[j].T), bf(w_k), bf(w_v.T))
            attn = _attention(qt, k, vt, rel_bias, diff_lambda[j], g_subln[j][:, None],
                              lambda_init).reshape(b * s, d)
        gains = jnp.stack([g_pre_ffn[i], g_post_ffn[i], g_pre_ple[i], g_post_ple[i], g_post_mix[i]])
        x2 = _ffn_ple(x2, p2[i], gains, bf(w_ffn_up[i]), w_ffn_conv[i], bf(w_ffn_down[i]),
                      bf(w_ple_gate[i]), bf(w_ple_proj[i]), s,
                      attn=attn, w_o=None if attn is None else bf(w_o[i - n_a]))
    return x2.reshape(b, s, d)
```

```python
import functools
import math

import numpy as np
import jax
import jax.numpy as jnp
from jax import lax
from jax.experimental import pallas as pl
from jax.experimental.pallas import tpu as pltpu

F32 = jnp.float32
BF16 = jnp.bfloat16

EPS = 1e-6
LOG2E = math.log2(math.e)
CONV_WIDTH = 3
N_HEADS = 8
HEAD_DIM = 64
V_HEAD_DIM = 2 * HEAD_DIM
N_BUCKETS = 32
MAX_DISTANCE = 128

V7X_SUBLANES = 8
V7X_MXU_WIDTH = 256
V7X_VMEM_LIMIT = 56 * 1024 * 1024

TOKEN_TILE = 256
ATTN_TILE = 512
HALO = V7X_SUBLANES


def _dot(a, b):
    return jnp.dot(a, b, preferred_element_type=F32)


def _dot_nt(a, b):
    return lax.dot_general(a, b, (((1,), (1,)), ((), ())), preferred_element_type=F32)


def _normalize(x):
    return x * lax.rsqrt(jnp.mean(x * x, axis=-1, keepdims=True) + EPS)


def _rms(x, g):
    return _normalize(x) * g


def _sigmoid(x):
    return 1.0 / (1.0 + jnp.exp(-x))


def _causal_conv_cols(buf, y, wconv_ref, cols, tm):
    buf[HALO:HALO + tm, cols] = y
    return (wconv_ref[0:1, cols] * buf[HALO - 2:HALO - 2 + tm, cols]
            + wconv_ref[1:2, cols] * buf[HALO - 1:HALO - 1 + tm, cols]
            + wconv_ref[2:3, cols] * y)


def _resident(shape):
    return pl.BlockSpec(shape, lambda *_: (0,) * len(shape), pipeline_mode=pl.Buffered(1))


def _mixer_kernel(x_ref, gpre_ref, gpost_ref, win_ref, wconv_ref, wout_ref, o_ref,
                  zbuf, mbuf, *, tm, tiles_per_seq, d):
    i = pl.program_id(0)

    @pl.when(lax.rem(i, tiles_per_seq) == 0)
    def _():
        zbuf[0:HALO, :] = jnp.zeros((HALO, d), F32)

    x = x_ref[...]
    h = _rms(x, gpre_ref[...]).astype(BF16)
    cw = V7X_MXU_WIDTH
    for c in range(d // cw):
        cols = slice(c * cw, (c + 1) * cw)
        gb = _dot(h, win_ref[:, c * cw:(c + 1) * cw])
        gc = _dot(h, win_ref[:, d + c * cw:d + (c + 1) * cw])
        u = _dot(h, win_ref[:, 2 * d + c * cw:2 * d + (c + 1) * cw])
        conv = _causal_conv_cols(zbuf, gc * u, wconv_ref, cols, tm)
        mbuf[:, cols] = (gb * conv).astype(BF16)
    zbuf[0:HALO, :] = zbuf[tm:tm + HALO, :]
    mix = _dot(mbuf[...], wout_ref[...])
    o_ref[...] = x + _rms(mix, gpost_ref[...])


def _mixer(x2, g_pre, g_post, w_in, w_conv, w_out, seq):
    t, d = x2.shape
    tm = TOKEN_TILE
    kern = functools.partial(_mixer_kernel, tm=tm, tiles_per_seq=seq // tm, d=d)
    row = pl.BlockSpec((tm, d), lambda i: (i, 0))
    return pl.pallas_call(
        kern,
        grid=(t // tm,),
        in_specs=[row, _resident((1, d)), _resident((1, d)), _resident((d, 3 * d)),
                  _resident((CONV_WIDTH, d)), _resident((d, d))],
        out_specs=row,
        out_shape=jax.ShapeDtypeStruct((t, d), F32),
        scratch_shapes=[pltpu.VMEM((tm + HALO, d), F32), pltpu.VMEM((tm, d), BF16)],
        compiler_params=pltpu.CompilerParams(
            dimension_semantics=("arbitrary",), vmem_limit_bytes=V7X_VMEM_LIMIT),
        name="sconv_mixer",
    )(x2, g_pre, g_post, w_in, w_conv, w_out)


def _ffn_ple_kernel(*refs, tm, tiles_per_seq, d, f, with_attn_out):
    if with_attn_out:
        (x_ref, p_ref, a_ref, gains_ref, wo_ref, wup_ref, wconv_ref, wdown_ref, wgate_ref,
         wproj_ref, o_ref, ybuf, abuf) = refs
    else:
        (x_ref, p_ref, gains_ref, wup_ref, wconv_ref, wdown_ref, wgate_ref,
         wproj_ref, o_ref, ybuf, abuf) = refs
    i = pl.program_id(0)

    @pl.when(lax.rem(i, tiles_per_seq) == 0)
    def _():
        ybuf[0:HALO, :] = jnp.zeros((HALO, 2 * f), F32)

    x = x_ref[...]
    if with_attn_out:
        x = x + _rms(_dot(a_ref[...], wo_ref[...]), gains_ref[4:5, :])

    hn = _rms(x, gains_ref[0:1, :]).astype(BF16)
    cw = V7X_MXU_WIDTH
    for c in range(f // cw):
        gcols = slice(c * cw, (c + 1) * cw)
        ucols = slice(f + c * cw, f + (c + 1) * cw)
        gate = _causal_conv_cols(ybuf, _dot(hn, wup_ref[:, c * cw:(c + 1) * cw]), wconv_ref, gcols, tm)
        up = _causal_conv_cols(ybuf, _dot(hn, wup_ref[:, f + c * cw:f + (c + 1) * cw]), wconv_ref, ucols, tm)
        abuf[:, gcols] = (gate * _sigmoid(gate) * up).astype(BF16)
    ybuf[0:HALO, :] = ybuf[tm:tm + HALO, :]
    x = x + _rms(_dot(abuf[...], wdown_ref[...]), gains_ref[1:2, :])

    gate = _sigmoid(_dot(_rms(x, gains_ref[2:3, :]).astype(BF16), wgate_ref[...]))
    emb = _dot(p_ref[...].astype(BF16), wproj_ref[...])
    o_ref[...] = x + _rms(gate * emb, gains_ref[3:4, :])


def _ffn_ple(x2, p2, gains, w_up, w_conv, w_down, w_gate, w_proj, seq, attn=None, w_o=None):
    t, d = x2.shape
    f = w_down.shape[0]
    pd = p2.shape[1]
    tm = TOKEN_TILE
    with_attn_out = attn is not None
    kern = functools.partial(_ffn_ple_kernel, tm=tm, tiles_per_seq=seq // tm, d=d, f=f,
                             with_attn_out=with_attn_out)
    row = pl.BlockSpec((tm, d), lambda i: (i, 0))
    prow = pl.BlockSpec((tm, pd), lambda i: (i, 0))
    args = [x2, p2]
    specs = [row, prow]
    if with_attn_out:
        args.append(attn)
        specs.append(row)
    args.append(gains)
    specs.append(_resident(gains.shape))
    if with_attn_out:
        args.append(w_o)
        specs.append(_resident((d, d)))
    args += [w_up, w_conv, w_down, w_gate, w_proj]
    specs += [_resident((d, 2 * f)), _resident((CONV_WIDTH, 2 * f)), _resident((f, d)),
              _resident((d, d)), _resident((pd, d))]
    return pl.pallas_call(
        kern,
        grid=(t // tm,),
        in_specs=specs,
        out_specs=row,
        out_shape=jax.ShapeDtypeStruct((t, d), F32),
        scratch_shapes=[pltpu.VMEM((tm + HALO, 2 * f), F32), pltpu.VMEM((tm, f), BF16)],
        compiler_params=pltpu.CompilerParams(
            dimension_semantics=("arbitrary",), vmem_limit_bytes=V7X_VMEM_LIMIT),
        name="attn_out_ffn_ple" if with_attn_out else "ffn_ple",
    )(*args)


def _qkv_kernel(x_ref, gq_ref, gkv_ref, wqt_ref, wk_ref, wvt_ref, qt_ref, k_ref, vt_ref):
    xn = _normalize(x_ref[0])
    h = (xn * gq_ref[...]).astype(BF16)
    hk = (xn * gkv_ref[...]).astype(BF16)
    qt_ref[0] = (_dot_nt(wqt_ref[...], h) * (HEAD_DIM ** -0.5 * LOG2E)).astype(BF16)
    k_ref[0] = _dot(hk, wk_ref[...]).astype(BF16)
    vt_ref[0] = _dot_nt(wvt_ref[...], hk).astype(BF16)


def _qkv(x3, g_q, g_kv, wq_t, w_k, wv_t):
    b, s, d = x3.shape
    tm = TOKEN_TILE
    row = pl.BlockSpec((1, tm, d), lambda bi, i: (bi, i, 0))
    col = pl.BlockSpec((1, d, tm), lambda bi, i: (bi, 0, i))
    return pl.pallas_call(
        _qkv_kernel,
        grid=(b, s // tm),
        in_specs=[row, _resident((1, d)), _resident((1, d)), _resident((d, d)),
                  _resident((d, d)), _resident((d, d))],
        out_specs=[col, row, col],
        out_shape=[jax.ShapeDtypeStruct((b, d, s), BF16), jax.ShapeDtypeStruct((b, s, d), BF16),
                   jax.ShapeDtypeStruct((b, d, s), BF16)],
        compiler_params=pltpu.CompilerParams(
            dimension_semantics=("arbitrary", "arbitrary"), vmem_limit_bytes=V7X_VMEM_LIMIT),
        name="qkv_proj",
    )(x3, g_q, g_kv, wq_t, w_k, wv_t)


def _bucket_thresholds(seq):
    dist = np.arange(seq)
    max_exact = N_BUCKETS // 2
    large = max_exact + (np.log(np.maximum(dist, 1).astype(np.float32) / np.float32(max_exact))
                         / np.float32(math.log(MAX_DISTANCE / max_exact))
                         * np.float32(N_BUCKETS - max_exact)).astype(np.int32)
    bucket = np.where(dist < max_exact, dist, np.minimum(large, N_BUCKETS - 1))
    assert np.all(np.diff(bucket) >= 0)
    thr = [int(np.argmax(bucket >= b)) if np.any(bucket >= b) else seq for b in range(N_BUCKETS)]
    assert np.all(bucket[MAX_DISTANCE:] == N_BUCKETS - 1)
    return thr


def _attn_kernel(rb_ref, qt_ref, k_ref, vt_ref, lam_ref, g_ref, o_ref,
                 bias_sc, q_sc, s_sc, m_sc, l_sc, acc_sc, *, t, thresholds, lambda_init):
    h = pl.program_id(0)
    bi = pl.program_id(1)
    qi = pl.program_id(2)

    @pl.when((bi == 0) & (qi == 0))
    def _():
        row = lax.broadcasted_iota(jnp.int32, (t, t), 0)
        col = lax.broadcasted_iota(jnp.int32, (t, t), 1)
        for tile, delta in ((0, 0), (1, t)):
            dist = col - row + delta
            val = jnp.full((t, t), rb_ref[0, h], F32)
            for b in range(1, N_BUCKETS):
                val = jnp.where(dist >= thresholds[b], rb_ref[b, h], val)
            val = val * LOG2E
            if tile == 0:
                val = jnp.where(dist >= 0, val, -jnp.inf)
            bias_sc[tile] = val

    zero = jnp.zeros((HEAD_DIM, t), BF16)
    q_sc[0, 0:HEAD_DIM, :] = qt_ref[0, 0:HEAD_DIM, :]
    q_sc[0, HEAD_DIM:, :] = zero
    q_sc[1, 0:HEAD_DIM, :] = zero
    q_sc[1, HEAD_DIM:, :] = qt_ref[0, HEAD_DIM:, :]

    m_sc[...] = jnp.full(m_sc.shape, -jnp.inf, F32)
    l_sc[...] = jnp.zeros(l_sc.shape, F32)
    acc_sc[...] = jnp.zeros(acc_sc.shape, F32)

    def scores(slot, kstart, bias_tile):
        kblk = k_ref[0, pl.ds(kstart, t), :]
        for mp in range(2):
            s = _dot(kblk, q_sc[mp])
            if bias_tile is not None:
                s = s + bias_sc[bias_tile]
            s_sc[slot, mp] = s

    def softmax_pv(slot, kstart, shift):
        vtblk = vt_ref[0, :, pl.ds(kstart, t)]
        for mp in range(2):
            m_old = m_sc[mp]
            m_new = jnp.maximum(m_old, jnp.max(s_sc[slot, mp], axis=0, keepdims=True) + shift)
            alpha = jnp.exp2(m_old - m_new)
            p = jnp.exp2(s_sc[slot, mp] - (m_new - shift))
            l_sc[mp] = alpha * l_sc[mp] + jnp.sum(p, axis=0, keepdims=True)
            acc_sc[mp] = alpha * acc_sc[mp] + _dot(vtblk, p.astype(BF16))
            m_sc[mp] = m_new

    far_shift = rb_ref[N_BUCKETS - 1, h] * LOG2E
    kstart = lambda n: pl.multiple_of((qi - n) * t, t)

    shift_of = lambda n: jnp.where(n <= 1, 0.0, far_shift)
    qi_even = lax.rem(qi, 2) == 0

    scores(0, kstart(0), 0)

    @pl.when(qi >= 1)
    def _():
        scores(1, kstart(1), 1)
        softmax_pv(0, kstart(0), 0.0)

    def far_pair(pair, carry):
        n = 2 + 2 * pair
        scores(0, kstart(n), None)
        softmax_pv(1, kstart(n - 1), shift_of(n - 1))
        scores(1, kstart(n + 1), None)
        softmax_pv(0, kstart(n), far_shift)
        return carry

    lax.fori_loop(0, lax.shift_right_arithmetic(qi - 1, 1), far_pair, 0)

    @pl.when(qi_even & (qi >= 2))
    def _():
        scores(0, kstart(qi), None)
        softmax_pv(1, kstart(qi - 1), shift_of(qi - 1))

    @pl.when(qi_even)
    def _():
        softmax_pv(0, kstart(qi), shift_of(qi))

    @pl.when(jnp.logical_not(qi_even))
    def _():
        softmax_pv(1, kstart(qi), shift_of(qi))

    lv = lam_ref[...]
    lam = (jnp.exp(jnp.sum(lv[0:1] * lv[1:2], axis=-1, keepdims=True))
           - jnp.exp(jnp.sum(lv[2:3] * lv[3:4], axis=-1, keepdims=True)) + lambda_init)
    o = acc_sc[0] * (1.0 / l_sc[0]) - acc_sc[1] * (lam / l_sc[1])
    o = o * lax.rsqrt(jnp.mean(o * o, axis=0, keepdims=True) + EPS) * g_ref[...]
    o = o * (1.0 - lambda_init)
    o_ref[0] = o.T.astype(BF16)


def _attention(qt, k, vt, rel_bias, lam_vecs, g_subln, lambda_init):
    b, s, _ = k.shape
    t = ATTN_TILE
    assert s % t == 0 and t >= MAX_DISTANCE
    hd = V_HEAD_DIM
    kern = functools.partial(_attn_kernel, t=t, thresholds=_bucket_thresholds(s),
                             lambda_init=lambda_init)
    return pl.pallas_call(
        kern,
        grid=(N_HEADS, b, s // t),
        in_specs=[
            pl.BlockSpec(memory_space=pltpu.SMEM),
            pl.BlockSpec((1, hd, t), lambda h, bi, qi: (bi, h, qi)),
            pl.BlockSpec((1, s, hd), lambda h, bi, qi: (bi, 0, h)),
            pl.BlockSpec((1, hd, s), lambda h, bi, qi: (bi, h, 0)),
            _resident(lam_vecs.shape),
            _resident((hd, 1)),
        ],
        out_specs=pl.BlockSpec((1, t, hd), lambda h, bi, qi: (bi, qi, h)),
        out_shape=jax.ShapeDtypeStruct((b, s, N_HEADS * hd), BF16),
        scratch_shapes=[
            pltpu.VMEM((2, t, t), F32),
            pltpu.VMEM((2, hd, t), BF16),
            pltpu.VMEM((2, 2, t, t), F32),
            pltpu.VMEM((2, 1, t), F32),
            pltpu.VMEM((2, 1, t), F32),
            pltpu.VMEM((2, hd, t), F32),
        ],
        compiler_params=pltpu.CompilerParams(
            dimension_semantics=("arbitrary", "arbitrary", "arbitrary"),
            vmem_limit_bytes=V7X_VMEM_LIMIT),
        name="diff_attention",
    )(rel_bias, qt, k, vt, lam_vecs, g_subln)


def kernel(x, p, g_pre_mix, g_post_mix, g_pre_ffn, g_post_ffn, g_pre_ple, g_post_ple,
           w_sc_in, w_sc_conv, w_sc_out, g_kv, w_kv, rel_bias, w_q, diff_lambda,
           g_subln, w_o, w_ffn_up, w_ffn_conv, w_ffn_down, w_ple_gate, w_ple_proj):
    b, s, d = x.shape
    depth = p.shape[0]
    n_a = depth // 2
    qk_width = N_HEADS * 2 * HEAD_DIM
    x2 = x.reshape(b * s, d)
    p2 = p.reshape(depth, b * s, p.shape[-1])
    bf = lambda w: w.astype(BF16)

    for i in range(depth):
        attn = None
        if i < n_a:
            x2 = _mixer(x2, g_pre_mix[i][None], g_post_mix[i][None], bf(w_sc_in[i]),
                        w_sc_conv[i], bf(w_sc_out[i]), s)
        else:
            j = i - n_a
            assert j == 0, "one attention layer: K/V and Q read the same stream"
            w_k, w_v = w_kv[:, :qk_width], w_kv[:, qk_width:]
            lambda_init = 0.8 - 0.6 * math.exp(-0.3 * i)
            qt, k, vt = _qkv(x2.reshape(b, s, d), g_pre_mix[i][None], g_kv[None],
                             bf(w_q[j].T), bf(w_k), bf(w_v.T))
            attn = _attention(qt, k, vt, rel_bias, diff_lambda[j], g_subln[j][:, None],
                              lambda_init).reshape(b * s, d)
        gains = jnp.stack([g_pre_ffn[i], g_post_ffn[i], g_pre_ple[i], g_post_ple[i], g_post_mix[i]])
        x2 = _ffn_ple(x2, p2[i], gains, bf(w_ffn_up[i]), w_ffn_conv[i], bf(w_ffn_down[i]),
                      bf(w_ple_gate[i]), bf(w_ple_proj[i]), s,
                      attn=attn, w_o=None if attn is None else bf(w_o[i - n_a]))
    return x2.reshape(b, s, d)
```

```python
import functools
import math

import numpy as np
import jax
import jax.numpy as jnp
from jax import lax
from jax.experimental import pallas as pl
from jax.experimental.pallas import tpu as pltpu

F32 = jnp.float32
BF16 = jnp.bfloat16

EPS = 1e-6
LOG2E = math.log2(math.e)
CONV_WIDTH = 3
N_HEADS = 8
HEAD_DIM = 64
V_HEAD_DIM = 2 * HEAD_DIM
N_BUCKETS = 32
MAX_DISTANCE = 128

V7X_SUBLANES = 8
V7X_MXU_WIDTH = 256
V7X_VMEM_LIMIT = 56 * 1024 * 1024

TOKEN_TILE = 256
ATTN_TILE = 512
ATTN_STAGES_PER_TRIP = 12
HALO = V7X_SUBLANES
ONES_ROWS = 2 * V7X_SUBLANES


def _dot(a, b):
    return jnp.dot(a, b, preferred_element_type=F32)


def _dot_nt(a, b):
    return lax.dot_general(a, b, (((1,), (1,)), ((), ())), preferred_element_type=F32)


def _normalize(x):
    return x * lax.rsqrt(jnp.mean(x * x, axis=-1, keepdims=True) + EPS)


def _rms(x, g):
    return _normalize(x) * g


def _sigmoid(x):
    return 1.0 / (1.0 + jnp.exp(-x))


def _causal_conv_cols(buf, y, wconv_ref, cols, tm):
    buf[HALO:HALO + tm, cols] = y
    return (wconv_ref[0:1, cols] * buf[HALO - 2:HALO - 2 + tm, cols]
            + wconv_ref[1:2, cols] * buf[HALO - 1:HALO - 1 + tm, cols]
            + wconv_ref[2:3, cols] * y)


def _resident(shape):
    return pl.BlockSpec(shape, lambda *_: (0,) * len(shape), pipeline_mode=pl.Buffered(1))


def _mixer_kernel(x_ref, gpre_ref, gpost_ref, win_ref, wconv_ref, wout_ref, o_ref,
                  zbuf, mbuf, *, tm, tiles_per_seq, d):
    i = pl.program_id(0)

    @pl.when(lax.rem(i, tiles_per_seq) == 0)
    def _():
        zbuf[0:HALO, :] = jnp.zeros((HALO, d), F32)

    x = x_ref[...]
    h = _rms(x, gpre_ref[...]).astype(BF16)
    cw = V7X_MXU_WIDTH
    for c in range(d // cw):
        cols = slice(c * cw, (c + 1) * cw)
        gb = _dot(h, win_ref[:, c * cw:(c + 1) * cw])
        gc = _dot(h, win_ref[:, d + c * cw:d + (c + 1) * cw])
        u = _dot(h, win_ref[:, 2 * d + c * cw:2 * d + (c + 1) * cw])
        conv = _causal_conv_cols(zbuf, gc * u, wconv_ref, cols, tm)
        mbuf[:, cols] = (gb * conv).astype(BF16)
    zbuf[0:HALO, :] = zbuf[tm:tm + HALO, :]
    mix = _dot(mbuf[...], wout_ref[...])
    o_ref[...] = x + _rms(mix, gpost_ref[...])


def _mixer(x2, g_pre, g_post, w_in, w_conv, w_out, seq):
    t, d = x2.shape
    tm = TOKEN_TILE
    kern = functools.partial(_mixer_kernel, tm=tm, tiles_per_seq=seq // tm, d=d)
    row = pl.BlockSpec((tm, d), lambda i: (i, 0))
    return pl.pallas_call(
        kern,
        grid=(t // tm,),
        in_specs=[row, _resident((1, d)), _resident((1, d)), _resident((d, 3 * d)),
                  _resident((CONV_WIDTH, d)), _resident((d, d))],
        out_specs=row,
        out_shape=jax.ShapeDtypeStruct((t, d), F32),
        scratch_shapes=[pltpu.VMEM((tm + HALO, d), F32), pltpu.VMEM((tm, d), BF16)],
        compiler_params=pltpu.CompilerParams(
            dimension_semantics=("arbitrary",), vmem_limit_bytes=V7X_VMEM_LIMIT),
        name="sconv_mixer",
    )(x2, g_pre, g_post, w_in, w_conv, w_out)


def _ffn_ple_kernel(*refs, tm, tiles_per_seq, d, f, with_attn_out):
    if with_attn_out:
        (x_ref, p_ref, a_ref, gains_ref, wo_ref, wup_ref, wconv_ref, wdown_ref, wgate_ref,
         wproj_ref, o_ref, ybuf, abuf) = refs
    else:
        (x_ref, p_ref, gains_ref, wup_ref, wconv_ref, wdown_ref, wgate_ref,
         wproj_ref, o_ref, ybuf, abuf) = refs
    i = pl.program_id(0)

    @pl.when(lax.rem(i, tiles_per_seq) == 0)
    def _():
        ybuf[0:HALO, :] = jnp.zeros((HALO, 2 * f), F32)

    x = x_ref[...]
    if with_attn_out:
        x = x + _rms(_dot(a_ref[...], wo_ref[...]), gains_ref[4:5, :])

    hn = _rms(x, gains_ref[0:1, :]).astype(BF16)
    cw = V7X_MXU_WIDTH
    for c in range(f // cw):
        gcols = slice(c * cw, (c + 1) * cw)
        ucols = slice(f + c * cw, f + (c + 1) * cw)
        gate = _causal_conv_cols(ybuf, _dot(hn, wup_ref[:, c * cw:(c + 1) * cw]), wconv_ref, gcols, tm)
        up = _causal_conv_cols(ybuf, _dot(hn, wup_ref[:, f + c * cw:f + (c + 1) * cw]), wconv_ref, ucols, tm)
        abuf[:, gcols] = (gate * _sigmoid(gate) * up).astype(BF16)
    ybuf[0:HALO, :] = ybuf[tm:tm + HALO, :]
    x = x + _rms(_dot(abuf[...], wdown_ref[...]), gains_ref[1:2, :])

    gate = _sigmoid(_dot(_rms(x, gains_ref[2:3, :]).astype(BF16), wgate_ref[...]))
    emb = _dot(p_ref[...].astype(BF16), wproj_ref[...])
    o_ref[...] = x + _rms(gate * emb, gains_ref[3:4, :])


def _ffn_ple(x2, p2, gains, w_up, w_conv, w_down, w_gate, w_proj, seq, attn=None, w_o=None):
    t, d = x2.shape
    f = w_down.shape[0]
    pd = p2.shape[1]
    tm = TOKEN_TILE
    with_attn_out = attn is not None
    kern = functools.partial(_ffn_ple_kernel, tm=tm, tiles_per_seq=seq // tm, d=d, f=f,
                             with_attn_out=with_attn_out)
    row = pl.BlockSpec((tm, d), lambda i: (i, 0))
    prow = pl.BlockSpec((tm, pd), lambda i: (i, 0))
    args = [x2, p2]
    specs = [row, prow]
    if with_attn_out:
        args.append(attn)
        specs.append(row)
    args.append(gains)
    specs.append(_resident(gains.shape))
    if with_attn_out:
        args.append(w_o)
        specs.append(_resident((d, d)))
    args += [w_up, w_conv, w_down, w_gate, w_proj]
    specs += [_resident((d, 2 * f)), _resident((CONV_WIDTH, 2 * f)), _resident((f, d)),
              _resident((d, d)), _resident((pd, d))]
    return pl.pallas_call(
        kern,
        grid=(t // tm,),
        in_specs=specs,
        out_specs=row,
        out_shape=jax.ShapeDtypeStruct((t, d), F32),
        scratch_shapes=[pltpu.VMEM((tm + HALO, 2 * f), F32), pltpu.VMEM((tm, f), BF16)],
        compiler_params=pltpu.CompilerParams(
            dimension_semantics=("arbitrary",), vmem_limit_bytes=V7X_VMEM_LIMIT),
        name="attn_out_ffn_ple" if with_attn_out else "ffn_ple",
    )(*args)


def _qkv_kernel(x_ref, gq_ref, gkv_ref, wqt_ref, wk_ref, wvt_ref, qt_ref, k_ref, vt_ref):
    xn = _normalize(x_ref[0])
    h = (xn * gq_ref[...]).astype(BF16)
    hk = (xn * gkv_ref[...]).astype(BF16)
    qt_ref[0] = (_dot_nt(wqt_ref[...], h) * (HEAD_DIM ** -0.5 * LOG2E)).astype(BF16)
    k_ref[0] = _dot(hk, wk_ref[...]).astype(BF16)
    vt_ref[0] = _dot_nt(wvt_ref[...], hk).astype(BF16)


def _qkv(x3, g_q, g_kv, wq_t, w_k, wv_t):
    b, s, d = x3.shape
    tm = TOKEN_TILE
    row = pl.BlockSpec((1, tm, d), lambda bi, i: (bi, i, 0))
    col = pl.BlockSpec((1, d, tm), lambda bi, i: (bi, 0, i))
    return pl.pallas_call(
        _qkv_kernel,
        grid=(b, s // tm),
        in_specs=[row, _resident((1, d)), _resident((1, d)), _resident((d, d)),
                  _resident((d, d)), _resident((d, d))],
        out_specs=[col, row, col],
        out_shape=[jax.ShapeDtypeStruct((b, d, s), BF16), jax.ShapeDtypeStruct((b, s, d), BF16),
                   jax.ShapeDtypeStruct((b, d, s), BF16)],
        compiler_params=pltpu.CompilerParams(
            dimension_semantics=("arbitrary", "arbitrary"), vmem_limit_bytes=V7X_VMEM_LIMIT),
        name="qkv_proj",
    )(x3, g_q, g_kv, wq_t, w_k, wv_t)


def _bucket_thresholds(seq):
    dist = np.arange(seq)
    max_exact = N_BUCKETS // 2
    large = max_exact + (np.log(np.maximum(dist, 1).astype(np.float32) / np.float32(max_exact))
                         / np.float32(math.log(MAX_DISTANCE / max_exact))
                         * np.float32(N_BUCKETS - max_exact)).astype(np.int32)
    bucket = np.where(dist < max_exact, dist, np.minimum(large, N_BUCKETS - 1))
    assert np.all(np.diff(bucket) >= 0)
    thr = [int(np.argmax(bucket >= b)) if np.any(bucket >= b) else seq for b in range(N_BUCKETS)]
    assert np.all(bucket[MAX_DISTANCE:] == N_BUCKETS - 1)
    return thr


def _attn_pairs(nq):
    pairs = [(qi, qi - n, min(n, 2)) for qi in range(nq) for n in range(qi + 1)]
    pairs.append(pairs[-1])
    return np.asarray(pairs, np.int32).T


def _attn_kernel(rb_ref, pairs_ref, qt_ref, k_ref, vt_ref, lam_ref, g_ref, o_ref,
                 bias_sc, q_sc, v_sc, s_sc, mx_sc, m_sc, l_sc, acc_sc,
                 *, t, nq, thresholds, lambda_init):
    h = pl.program_id(0)
    bi = pl.program_id(1)

    @pl.when(bi == 0)
    def _():
        row = lax.broadcasted_iota(jnp.int32, (t, t), 0)
        col = lax.broadcasted_iota(jnp.int32, (t, t), 1)
        for tile, delta in ((0, 0), (1, t)):
            dist = col - row + delta
            val = jnp.full((t, t), rb_ref[0, h], F32)
            for b in range(1, N_BUCKETS):
                val = jnp.where(dist >= thresholds[b], rb_ref[b, h], val)
            val = val * LOG2E
            if tile == 0:
                val = jnp.where(dist >= 0, val, -jnp.inf)
            bias_sc[tile] = val
        bias_sc[2] = jnp.full((t, t), rb_ref[N_BUCKETS - 1, h] * LOG2E, F32)

    zero = jnp.zeros((HEAD_DIM, nq * t), BF16)
    q_sc[0, 0:HEAD_DIM, :] = qt_ref[0, 0:HEAD_DIM, :]
    q_sc[0, HEAD_DIM:, :] = zero
    q_sc[1, 0:HEAD_DIM, :] = zero
    q_sc[1, HEAD_DIM:, :] = qt_ref[0, HEAD_DIM:, :]

    v_sc[0:V_HEAD_DIM, :] = vt_ref[0]
    v_sc[V_HEAD_DIM:, :] = jnp.ones((ONES_ROWS, nq * t), BF16)

    m_sc[...] = jnp.full(m_sc.shape, -jnp.inf, F32)
    l_sc[...] = jnp.zeros(l_sc.shape, F32)
    acc_sc[...] = jnp.zeros(acc_sc.shape, F32)

    def scores(slot, i):
        qi, j, kind = pairs_ref[0, i], pairs_ref[1, i], pairs_ref[2, i]
        kblk = k_ref[0, pl.ds(pl.multiple_of(j * t, t), t), :]
        for mp in range(2):
            s = _dot(kblk, q_sc[mp, :, pl.ds(pl.multiple_of(qi * t, t), t)]) + bias_sc[kind]
            s_sc[slot, mp] = s
            mx_sc[slot, mp] = jnp.max(s, axis=0, keepdims=True)

    def softmax_pv(slot, i):
        qi, j = pairs_ref[0, i], pairs_ref[1, i]
        vblk = v_sc[:, pl.ds(pl.multiple_of(j * t, t), t)]
        for mp in range(2):
            m_old = m_sc[qi, mp]
            m_new = jnp.maximum(m_old, mx_sc[slot, mp])
            alpha = jnp.exp2(m_old - m_new)
            p = jnp.exp2((s_sc[slot, mp] - m_new).astype(BF16))
            pv = _dot(vblk, p)
            l_sc[qi, mp] = alpha * l_sc[qi, mp] + pv[V_HEAD_DIM:V_HEAD_DIM + 1]
            acc_sc[qi, mp] = alpha * acc_sc[qi, mp] + pv[0:V_HEAD_DIM]
            m_sc[qi, mp] = m_new

    n_pairs = pairs_ref.shape[1] - 1
    scores(0, 0)

    def stages(first, count):
        for u in range(count):
            scores((u + 1) % 2, first + u + 1)
            softmax_pv(u % 2, first + u)

    def trip_body(trip, carry):
        stages(ATTN_STAGES_PER_TRIP * trip, ATTN_STAGES_PER_TRIP)
        return carry

    lax.fori_loop(0, n_pairs // ATTN_STAGES_PER_TRIP, trip_body, 0)
    done = n_pairs - n_pairs % ATTN_STAGES_PER_TRIP
    stages(done, n_pairs - done)

    lv = lam_ref[...]
    lam = (jnp.exp(jnp.sum(lv[0:1] * lv[1:2], axis=-1, keepdims=True))
           - jnp.exp(jnp.sum(lv[2:3] * lv[3:4], axis=-1, keepdims=True)) + lambda_init)
    for qi in range(nq):
        o = acc_sc[qi, 0] * (1.0 / l_sc[qi, 0]) - acc_sc[qi, 1] * (lam / l_sc[qi, 1])
        o = o * lax.rsqrt(jnp.mean(o * o, axis=0, keepdims=True) + EPS) * g_ref[...]
        o = o * (1.0 - lambda_init)
        o_ref[0, qi * t:(qi + 1) * t, :] = o.T.astype(BF16)


def _attention(qt, k, vt, rel_bias, lam_vecs, g_subln, lambda_init):
    b, s, _ = k.shape
    t = ATTN_TILE
    assert s % t == 0 and t >= MAX_DISTANCE
    nq = s // t
    hd = V_HEAD_DIM
    pairs = _attn_pairs(nq)
    kern = functools.partial(_attn_kernel, t=t, nq=nq, thresholds=_bucket_thresholds(s),
                             lambda_init=lambda_init)
    return pl.pallas_call(
        kern,
        grid=(N_HEADS, b),
        in_specs=[
            pl.BlockSpec(memory_space=pltpu.SMEM),
            pl.BlockSpec(memory_space=pltpu.SMEM),
            pl.BlockSpec((1, hd, s), lambda h, bi: (bi, h, 0)),
            pl.BlockSpec((1, s, hd), lambda h, bi: (bi, 0, h)),
            pl.BlockSpec((1, hd, s), lambda h, bi: (bi, h, 0)),
            _resident(lam_vecs.shape),
            _resident((hd, 1)),
        ],
        out_specs=pl.BlockSpec((1, s, hd), lambda h, bi: (bi, 0, h)),
        out_shape=jax.ShapeDtypeStruct((b, s, N_HEADS * hd), BF16),
        scratch_shapes=[
            pltpu.VMEM((3, t, t), F32),
            pltpu.VMEM((2, hd, s), BF16),
            pltpu.VMEM((hd + ONES_ROWS, s), BF16),
            pltpu.VMEM((2, 2, t, t), F32),
            pltpu.VMEM((2, 2, 1, t), F32),
            pltpu.VMEM((nq, 2, 1, t), F32),
            pltpu.VMEM((nq, 2, 1, t), F32),
            pltpu.VMEM((nq, 2, hd, t), F32),
        ],
        compiler_params=pltpu.CompilerParams(
            dimension_semantics=("arbitrary", "arbitrary"),
            vmem_limit_bytes=V7X_VMEM_LIMIT),
        name="diff_attention",
    )(rel_bias, jnp.asarray(pairs), qt, k, vt, lam_vecs, g_subln)


def kernel(x, p, g_pre_mix, g_post_mix, g_pre_ffn, g_post_ffn, g_pre_ple, g_post_ple,
           w_sc_in, w_sc_conv, w_sc_out, g_kv, w_kv, rel_bias, w_q, diff_lambda,
           g_subln, w_o, w_ffn_up, w_ffn_conv, w_ffn_down, w_ple_gate, w_ple_proj):
    b, s, d = x.shape
    depth = p.shape[0]
    n_a = depth // 2
    qk_width = N_HEADS * 2 * HEAD_DIM
    x2 = x.reshape(b * s, d)
    p2 = p.reshape(depth, b * s, p.shape[-1])
    bf = lambda w: w.astype(BF16)

    for i in range(depth):
        attn = None
        if i < n_a:
            x2 = _mixer(x2, g_pre_mix[i][None], g_post_mix[i][None], bf(w_sc_in[i]),
                        w_sc_conv[i], bf(w_sc_out[i]), s)
        else:
            j = i - n_a
            assert j == 0, "one attention layer: K/V and Q read the same stream"
            w_k, w_v = w_kv[:, :qk_width], w_kv[:, qk_width:]
            lambda_init = 0.8 - 0.6 * math.exp(-0.3 * i)
            qt, k, vt = _qkv(x2.reshape(b, s, d), g_pre_mix[i][None], g_kv[None],
                             bf(w_q[j].T), bf(w_k), bf(w_v.T))
            attn = _attention(qt, k, vt, rel_bias, diff_lambda[j], g_subln[j][:, None],
                              lambda_init).reshape(b * s, d)
        gains = jnp.stack([g_pre_ffn[i], g_post_ffn[i], g_pre_ple[i], g_post_ple[i], g_post_mix[i]])
        x2 = _ffn_ple(x2, p2[i], gains, bf(w_ffn_up[i]), w_ffn_conv[i], bf(w_ffn_down[i]),
                      bf(w_ple_gate[i]), bf(w_ple_proj[i]), s,
                      attn=attn, w_o=None if attn is None else bf(w_o[i - n_a]))
    return x2.reshape(b, s, d)
```

```python
import functools
import math

import numpy as np
import jax
import jax.numpy as jnp
from jax import lax
from jax.experimental import pallas as pl
from jax.experimental.pallas import tpu as pltpu

F32 = jnp.float32
BF16 = jnp.bfloat16

EPS = 1e-6
LOG2E = math.log2(math.e)
CONV_WIDTH = 3
N_HEADS = 8
HEAD_DIM = 64
V_HEAD_DIM = 2 * HEAD_DIM
N_BUCKETS = 32
MAX_DISTANCE = 128

V7X_SUBLANES = 8
V7X_MXU_WIDTH = 256
V7X_VMEM_LIMIT = 56 * 1024 * 1024

TOKEN_TILE = 512
SUB_TILE = 256
ATTN_TILE = 512
ATTN_STAGES_PER_TRIP = 12
HALO = V7X_SUBLANES
ONES_ROWS = 2 * V7X_SUBLANES


def _dot(a, b):
    return jnp.dot(a, b, preferred_element_type=F32)


def _dot_nt(a, b):
    return lax.dot_general(a, b, (((1,), (1,)), ((), ())), preferred_element_type=F32)


def _normalize(x):
    return x * lax.rsqrt(jnp.mean(x * x, axis=-1, keepdims=True) + EPS)


def _rms(x, g):
    return _normalize(x) * g


def _sigmoid(x):
    return 1.0 / (1.0 + jnp.exp(-x))


def _causal_conv_cols(buf, y, wconv_ref, cols, r0):
    rows = y.shape[0]
    buf[HALO + r0:HALO + r0 + rows, cols] = y
    return (wconv_ref[0:1, cols] * buf[HALO + r0 - 2:HALO + r0 - 2 + rows, cols]
            + wconv_ref[1:2, cols] * buf[HALO + r0 - 1:HALO + r0 - 1 + rows, cols]
            + wconv_ref[2:3, cols] * y)


def _emit_staggered(chains):
    depth = max(len(chain) for chain in chains)
    for step in range(depth + len(chains) - 1):
        for lag, chain in enumerate(chains):
            if 0 <= step - lag < len(chain):
                chain[step - lag]()


def _resident(shape):
    return pl.BlockSpec(shape, lambda *_: (0,) * len(shape), pipeline_mode=pl.Buffered(1))


def _mixer_kernel(x_ref, gpre_ref, gpost_ref, win_ref, wconv_ref, wout_ref, o_ref,
                  zbuf, mbuf, *, tm, tiles_per_seq, d):
    i = pl.program_id(0)

    @pl.when(lax.rem(i, tiles_per_seq) == 0)
    def _():
        zbuf[0:HALO, :] = jnp.zeros((HALO, d), F32)

    cw = V7X_MXU_WIDTH

    def chain(r0):
        rows = slice(r0, r0 + SUB_TILE)
        v = {}

        def pre():
            v["h"] = _rms(x_ref[rows, :], gpre_ref[...]).astype(BF16)

        def chunk(c):
            cols = slice(c * cw, (c + 1) * cw)
            gb = _dot(v["h"], win_ref[:, c * cw:(c + 1) * cw])
            gc = _dot(v["h"], win_ref[:, d + c * cw:d + (c + 1) * cw])
            u = _dot(v["h"], win_ref[:, 2 * d + c * cw:2 * d + (c + 1) * cw])
            conv = _causal_conv_cols(zbuf, gc * u, wconv_ref, cols, r0)
            mbuf[rows, cols] = (gb * conv).astype(BF16)

        def out_proj():
            v["mix"] = _dot(mbuf[rows, :], wout_ref[...])

        def post():
            o_ref[rows, :] = x_ref[rows, :] + _rms(v["mix"], gpost_ref[...])

        return [pre] + [functools.partial(chunk, c) for c in range(d // cw)] + [out_proj, post]

    _emit_staggered([chain(r0) for r0 in range(0, tm, SUB_TILE)])
    zbuf[0:HALO, :] = zbuf[tm:tm + HALO, :]


def _mixer(x2, g_pre, g_post, w_in, w_conv, w_out, seq):
    t, d = x2.shape
    tm = TOKEN_TILE
    kern = functools.partial(_mixer_kernel, tm=tm, tiles_per_seq=seq // tm, d=d)
    row = pl.BlockSpec((tm, d), lambda i: (i, 0))
    return pl.pallas_call(
        kern,
        grid=(t // tm,),
        in_specs=[row, _resident((1, d)), _resident((1, d)), _resident((d, 3 * d)),
                  _resident((CONV_WIDTH, d)), _resident((d, d))],
        out_specs=row,
        out_shape=jax.ShapeDtypeStruct((t, d), F32),
        scratch_shapes=[pltpu.VMEM((tm + HALO, d), F32), pltpu.VMEM((tm, d), BF16)],
        compiler_params=pltpu.CompilerParams(
            dimension_semantics=("arbitrary",), vmem_limit_bytes=V7X_VMEM_LIMIT),
        name="sconv_mixer",
    )(x2, g_pre, g_post, w_in, w_conv, w_out)


def _ffn_ple_kernel(*refs, tm, tiles_per_seq, d, f, with_attn_out):
    if with_attn_out:
        (x_ref, p_ref, a_ref, gains_ref, wo_ref, wup_ref, wconv_ref, wdown_ref, wgate_ref,
         wproj_ref, o_ref, ybuf, abuf) = refs
    else:
        (x_ref, p_ref, gains_ref, wup_ref, wconv_ref, wdown_ref, wgate_ref,
         wproj_ref, o_ref, ybuf, abuf) = refs
    i = pl.program_id(0)

    @pl.when(lax.rem(i, tiles_per_seq) == 0)
    def _():
        ybuf[0:HALO, :] = jnp.zeros((HALO, 2 * f), F32)

    cw = V7X_MXU_WIDTH

    def chain(r0):
        rows = slice(r0, r0 + SUB_TILE)
        v = {}

        def pre():
            x = x_ref[rows, :]
            if with_attn_out:
                x = x + _rms(_dot(a_ref[rows, :], wo_ref[...]), gains_ref[4:5, :])
            v["x"] = x
            v["hn"] = _rms(x, gains_ref[0:1, :]).astype(BF16)

        def chunk(c):
            gcols = slice(c * cw, (c + 1) * cw)
            ucols = slice(f + c * cw, f + (c + 1) * cw)
            gate = _causal_conv_cols(ybuf, _dot(v["hn"], wup_ref[:, c * cw:(c + 1) * cw]),
                                     wconv_ref, gcols, r0)
            up = _causal_conv_cols(ybuf, _dot(v["hn"], wup_ref[:, f + c * cw:f + (c + 1) * cw]),
                                   wconv_ref, ucols, r0)
            abuf[rows, gcols] = (gate * _sigmoid(gate) * up).astype(BF16)

        def down():
            v["ffn"] = _dot(abuf[rows, :], wdown_ref[...])

        def post_ffn():
            v["x"] = v["x"] + _rms(v["ffn"], gains_ref[1:2, :])
            v["hg"] = _rms(v["x"], gains_ref[2:3, :]).astype(BF16)

        def embed():
            v["gate"] = _dot(v["hg"], wgate_ref[...])
            v["emb"] = _dot(p_ref[rows, :].astype(BF16), wproj_ref[...])

        def post_embed():
            o_ref[rows, :] = v["x"] + _rms(_sigmoid(v["gate"]) * v["emb"], gains_ref[3:4, :])

        return ([pre] + [functools.partial(chunk, c) for c in range(f // cw)]
                + [down, post_ffn, embed, post_embed])

    _emit_staggered([chain(r0) for r0 in range(0, tm, SUB_TILE)])
    ybuf[0:HALO, :] = ybuf[tm:tm + HALO, :]


def _ffn_ple(x2, p3, layer, gains, w_up, w_conv, w_down, w_gate, w_proj, seq, attn=None, w_o=None):
    t, d = x2.shape
    f = w_down.shape[0]
    pd = p3.shape[2]
    tm = TOKEN_TILE
    with_attn_out = attn is not None
    kern = functools.partial(_ffn_ple_kernel, tm=tm, tiles_per_seq=seq // tm, d=d, f=f,
                             with_attn_out=with_attn_out)
    row = pl.BlockSpec((tm, d), lambda i: (i, 0))
    prow = pl.BlockSpec((None, tm, pd), lambda i: (layer, i, 0))
    args = [x2, p3]
    specs = [row, prow]
    if with_attn_out:
        args.append(attn)
        specs.append(row)
    args.append(gains)
    specs.append(_resident(gains.shape))
    if with_attn_out:
        args.append(w_o)
        specs.append(_resident((d, d)))
    args += [w_up, w_conv, w_down, w_gate, w_proj]
    specs += [_resident((d, 2 * f)), _resident((CONV_WIDTH, 2 * f)), _resident((f, d)),
              _resident((d, d)), _resident((pd, d))]
    return pl.pallas_call(
        kern,
        grid=(t // tm,),
        in_specs=specs,
        out_specs=row,
        out_shape=jax.ShapeDtypeStruct((t, d), F32),
        scratch_shapes=[pltpu.VMEM((tm + HALO, 2 * f), F32), pltpu.VMEM((tm, f), BF16)],
        compiler_params=pltpu.CompilerParams(
            dimension_semantics=("arbitrary",), vmem_limit_bytes=V7X_VMEM_LIMIT),
        name="attn_out_ffn_ple" if with_attn_out else "ffn_ple",
    )(*args)


def _qkv_kernel(x_ref, gq_ref, gkv_ref, wqt_ref, wk_ref, wvt_ref, qt_ref, k_ref, vt_ref):
    def chain(r0):
        rows = slice(r0, r0 + SUB_TILE)
        v = {}

        def pre():
            xn = _normalize(x_ref[0, rows, :])
            v["h"] = (xn * gq_ref[...]).astype(BF16)
            v["hk"] = (xn * gkv_ref[...]).astype(BF16)

        def q_proj():
            qt_ref[0, :, rows] = (_dot_nt(wqt_ref[...], v["h"]) * (HEAD_DIM ** -0.5 * LOG2E)).astype(BF16)

        def k_proj():
            k_ref[0, rows, :] = _dot(v["hk"], wk_ref[...]).astype(BF16)

        def v_proj():
            vt_ref[0, :, rows] = _dot_nt(wvt_ref[...], v["hk"]).astype(BF16)

        return [pre, q_proj, k_proj, v_proj]

    _emit_staggered([chain(r0) for r0 in range(0, x_ref.shape[1], SUB_TILE)])


def _qkv(x3, g_q, g_kv, wq_t, w_k, wv_t):
    b, s, d = x3.shape
    tm = TOKEN_TILE
    row = pl.BlockSpec((1, tm, d), lambda bi, i: (bi, i, 0))
    col = pl.BlockSpec((1, d, tm), lambda bi, i: (bi, 0, i))
    return pl.pallas_call(
        _qkv_kernel,
        grid=(b, s // tm),
        in_specs=[row, _resident((1, d)), _resident((1, d)), _resident((d, d)),
                  _resident((d, d)), _resident((d, d))],
        out_specs=[col, row, col],
        out_shape=[jax.ShapeDtypeStruct((b, d, s), BF16), jax.ShapeDtypeStruct((b, s, d), BF16),
                   jax.ShapeDtypeStruct((b, d, s), BF16)],
        compiler_params=pltpu.CompilerParams(
            dimension_semantics=("arbitrary", "arbitrary"), vmem_limit_bytes=V7X_VMEM_LIMIT),
        name="qkv_proj",
    )(x3, g_q, g_kv, wq_t, w_k, wv_t)


def _bucket_thresholds(seq):
    dist = np.arange(seq)
    max_exact = N_BUCKETS // 2
    large = max_exact + (np.log(np.maximum(dist, 1).astype(np.float32) / np.float32(max_exact))
                         / np.float32(math.log(MAX_DISTANCE / max_exact))
                         * np.float32(N_BUCKETS - max_exact)).astype(np.int32)
    bucket = np.where(dist < max_exact, dist, np.minimum(large, N_BUCKETS - 1))
    assert np.all(np.diff(bucket) >= 0)
    thr = [int(np.argmax(bucket >= b)) if np.any(bucket >= b) else seq for b in range(N_BUCKETS)]
    assert np.all(bucket[MAX_DISTANCE:] == N_BUCKETS - 1)
    return thr


def _attn_pairs(nq):
    pairs = [(qi, qi - n, min(n, 2)) for qi in range(nq) for n in range(qi + 1)]
    pairs.append(pairs[-1])
    return np.asarray(pairs, np.int32).T


def _attn_kernel(rb_ref, pairs_ref, qt_ref, k_ref, vt_ref, lam_ref, g_ref, o_ref,
                 bias_sc, q_sc, v_sc, s_sc, mx_sc, m_sc, l_sc, acc_sc,
                 *, t, nq, thresholds, lambda_init):
    h = pl.program_id(0)
    bi = pl.program_id(1)

    @pl.when(bi == 0)
    def _():
        row = lax.broadcasted_iota(jnp.int32, (t, t), 0)
        col = lax.broadcasted_iota(jnp.int32, (t, t), 1)
        for tile, delta in ((0, 0), (1, t)):
            dist = col - row + delta
            val = jnp.full((t, t), rb_ref[0, h], F32)
            for b in range(1, N_BUCKETS):
                val = jnp.where(dist >= thresholds[b], rb_ref[b, h], val)
            val = val * LOG2E
            if tile == 0:
                val = jnp.where(dist >= 0, val, -jnp.inf)
            bias_sc[tile] = val
        bias_sc[2] = jnp.full((t, t), rb_ref[N_BUCKETS - 1, h] * LOG2E, F32)

    zero = jnp.zeros((HEAD_DIM, nq * t), BF16)
    q_sc[0, 0:HEAD_DIM, :] = qt_ref[0, 0:HEAD_DIM, :]
    q_sc[0, HEAD_DIM:, :] = zero
    q_sc[1, 0:HEAD_DIM, :] = zero
    q_sc[1, HEAD_DIM:, :] = qt_ref[0, HEAD_DIM:, :]

    v_sc[0:V_HEAD_DIM, :] = vt_ref[0]
    v_sc[V_HEAD_DIM:, :] = jnp.ones((ONES_ROWS, nq * t), BF16)

    m_sc[...] = jnp.full(m_sc.shape, -jnp.inf, F32)
    l_sc[...] = jnp.zeros(l_sc.shape, F32)
    acc_sc[...] = jnp.zeros(acc_sc.shape, F32)

    def scores(slot, i):
        qi, j, kind = pairs_ref[0, i], pairs_ref[1, i], pairs_ref[2, i]
        kblk = k_ref[0, pl.ds(pl.multiple_of(j * t, t), t), :]
        for mp in range(2):
            s = _dot(kblk, q_sc[mp, :, pl.ds(pl.multiple_of(qi * t, t), t)]) + bias_sc[kind]
            s_sc[slot, mp] = s
            mx_sc[slot, mp] = jnp.max(s, axis=0, keepdims=True)

    def softmax_pv(slot, i):
        qi, j = pairs_ref[0, i], pairs_ref[1, i]
        vblk = v_sc[:, pl.ds(pl.multiple_of(j * t, t), t)]
        for mp in range(2):
            m_old = m_sc[qi, mp]
            m_new = jnp.maximum(m_old, mx_sc[slot, mp])
            alpha = jnp.exp2(m_old - m_new)
            p = jnp.exp2((s_sc[slot, mp] - m_new).astype(BF16))
            pv = _dot(vblk, p)
            l_sc[qi, mp] = alpha * l_sc[qi, mp] + pv[V_HEAD_DIM:V_HEAD_DIM + 1]
            acc_sc[qi, mp] = alpha * acc_sc[qi, mp] + pv[0:V_HEAD_DIM]
            m_sc[qi, mp] = m_new

    n_pairs = pairs_ref.shape[1] - 1
    scores(0, 0)

    def stages(first, count):
        for u in range(count):
            scores((u + 1) % 2, first + u + 1)
            softmax_pv(u % 2, first + u)

    def trip_body(trip, carry):
        stages(ATTN_STAGES_PER_TRIP * trip, ATTN_STAGES_PER_TRIP)
        return carry

    lax.fori_loop(0, n_pairs // ATTN_STAGES_PER_TRIP, trip_body, 0)
    done = n_pairs - n_pairs % ATTN_STAGES_PER_TRIP
    stages(done, n_pairs - done)

    lv = lam_ref[...]
    lam = (jnp.exp(jnp.sum(lv[0:1] * lv[1:2], axis=-1, keepdims=True))
           - jnp.exp(jnp.sum(lv[2:3] * lv[3:4], axis=-1, keepdims=True)) + lambda_init)
    for qi in range(nq):
        o = acc_sc[qi, 0] * (1.0 / l_sc[qi, 0]) - acc_sc[qi, 1] * (lam / l_sc[qi, 1])
        o = o * lax.rsqrt(jnp.mean(o * o, axis=0, keepdims=True) + EPS) * g_ref[...]
        o = o * (1.0 - lambda_init)
        o_ref[0, qi * t:(qi + 1) * t, :] = o.T.astype(BF16)


def _attention(qt, k, vt, rel_bias, lam_vecs, g_subln, lambda_init):
    b, s, _ = k.shape
    t = ATTN_TILE
    assert s % t == 0 and t >= MAX_DISTANCE
    nq = s // t
    hd = V_HEAD_DIM
    pairs = _attn_pairs(nq)
    kern = functools.partial(_attn_kernel, t=t, nq=nq, thresholds=_bucket_thresholds(s),
                             lambda_init=lambda_init)
    return pl.pallas_call(
        kern,
        grid=(N_HEADS, b),
        in_specs=[
            pl.BlockSpec(memory_space=pltpu.SMEM),
            pl.BlockSpec(memory_space=pltpu.SMEM),
            pl.BlockSpec((1, hd, s), lambda h, bi: (bi, h, 0)),
            pl.BlockSpec((1, s, hd), lambda h, bi: (bi, 0, h)),
            pl.BlockSpec((1, hd, s), lambda h, bi: (bi, h, 0)),
            _resident(lam_vecs.shape),
            _resident((hd, 1)),
        ],
        out_specs=pl.BlockSpec((1, s, hd), lambda h, bi: (bi, 0, h)),
        out_shape=jax.ShapeDtypeStruct((b, s, N_HEADS * hd), BF16),
        scratch_shapes=[
            pltpu.VMEM((3, t, t), F32),
            pltpu.VMEM((2, hd, s), BF16),
            pltpu.VMEM((hd + ONES_ROWS, s), BF16),
            pltpu.VMEM((2, 2, t, t), F32),
            pltpu.VMEM((2, 2, 1, t), F32),
            pltpu.VMEM((nq, 2, 1, t), F32),
            pltpu.VMEM((nq, 2, 1, t), F32),
            pltpu.VMEM((nq, 2, hd, t), F32),
        ],
        compiler_params=pltpu.CompilerParams(
            dimension_semantics=("arbitrary", "arbitrary"),
            vmem_limit_bytes=V7X_VMEM_LIMIT),
        name="diff_attention",
    )(rel_bias, jnp.asarray(pairs), qt, k, vt, lam_vecs, g_subln)


def kernel(x, p, g_pre_mix, g_post_mix, g_pre_ffn, g_post_ffn, g_pre_ple, g_post_ple,
           w_sc_in, w_sc_conv, w_sc_out, g_kv, w_kv, rel_bias, w_q, diff_lambda,
           g_subln, w_o, w_ffn_up, w_ffn_conv, w_ffn_down, w_ple_gate, w_ple_proj):
    b, s, d = x.shape
    depth = p.shape[0]
    n_a = depth // 2
    qk_width = N_HEADS * 2 * HEAD_DIM
    x2 = x.reshape(b * s, d)
    p2 = p.reshape(depth, b * s, p.shape[-1])
    bf = lambda w: w.astype(BF16)

    for i in range(depth):
        attn = None
        if i < n_a:
            x2 = _mixer(x2, g_pre_mix[i][None], g_post_mix[i][None], bf(w_sc_in[i]),
                        w_sc_conv[i], bf(w_sc_out[i]), s)
        else:
            j = i - n_a
            assert j == 0, "one attention layer: K/V and Q read the same stream"
            w_k, w_v = w_kv[:, :qk_width], w_kv[:, qk_width:]
            lambda_init = 0.8 - 0.6 * math.exp(-0.3 * i)
            qt, k, vt = _qkv(x2.reshape(b, s, d), g_pre_mix[i][None], g_kv[None],
                             bf(w_q[j].T), bf(w_k), bf(w_v.T))
            attn = _attention(qt, k, vt, rel_bias, diff_lambda[j], g_subln[j][:, None],
                              lambda_init).reshape(b * s, d)
        gains = jnp.stack([g_pre_ffn[i], g_post_ffn[i], g_pre_ple[i], g_post_ple[i], g_post_mix[i]])
        x2 = _ffn_ple(x2, p2, i, gains,bf(w_ffn_up[i]), w_ffn_conv[i], bf(w_ffn_down[i]),
                      bf(w_ple_gate[i]), bf(w_ple_proj[i]), s,
                      attn=attn, w_o=None if attn is None else bf(w_o[i - n_a]))
    return x2.reshape(b, s, d)
```

```python
import functools
import math

import numpy as np
import jax
import jax.numpy as jnp
from jax import lax
from jax.experimental import pallas as pl
from jax.experimental.pallas import tpu as pltpu

F32 = jnp.float32
BF16 = jnp.bfloat16

EPS = 1e-6
LOG2E = math.log2(math.e)
CONV_WIDTH = 3
N_HEADS = 8
HEAD_DIM = 64
V_HEAD_DIM = 2 * HEAD_DIM
N_BUCKETS = 32
MAX_DISTANCE = 128

V7X_SUBLANES = 8
V7X_MXU_WIDTH = 256
V7X_VMEM_LIMIT = 56 * 1024 * 1024

TOKEN_TILE = 512
SUB_TILE = 256
ATTN_TILE = 512
ATTN_STAGES_PER_TRIP = 12
HALO = V7X_SUBLANES
ONES_ROWS = 2 * V7X_SUBLANES


def _dot(a, b):
    return jnp.dot(a, b, preferred_element_type=F32)


def _dot_nt(a, b):
    return lax.dot_general(a, b, (((1,), (1,)), ((), ())), preferred_element_type=F32)


def _normalize(x):
    return x * lax.rsqrt(jnp.mean(x * x, axis=-1, keepdims=True) + EPS)


def _rms(x, g):
    return _normalize(x) * g


def _sigmoid(x):
    return 1.0 / (1.0 + jnp.exp(-x))


def _causal_conv_cols(buf, y, wconv_ref, cols, r0):
    rows = y.shape[0]
    buf[HALO + r0:HALO + r0 + rows, cols] = y
    return (wconv_ref[0:1, cols] * buf[HALO + r0 - 2:HALO + r0 - 2 + rows, cols]
            + wconv_ref[1:2, cols] * buf[HALO + r0 - 1:HALO + r0 - 1 + rows, cols]
            + wconv_ref[2:3, cols] * y)


def _emit_staggered(chains):
    depth = max(len(chain) for chain in chains)
    for step in range(depth + len(chains) - 1):
        for lag, chain in enumerate(chains):
            if 0 <= step - lag < len(chain):
                chain[step - lag]()


def _resident(shape, layer=None):
    if layer is None:
        return pl.BlockSpec(shape, lambda *_: (0,) * len(shape), pipeline_mode=pl.Buffered(1))
    return pl.BlockSpec((None,) + tuple(shape), lambda *_: (layer,) + (0,) * len(shape),
                        pipeline_mode=pl.Buffered(1))


def _mixer_kernel(x_ref, gpre_ref, gpost_ref, win_ref, wconv_ref, wout_ref, o_ref,
                  zbuf, mbuf, *, tm, tiles_per_seq, d):
    i = pl.program_id(0)

    @pl.when(lax.rem(i, tiles_per_seq) == 0)
    def _():
        zbuf[0:HALO, :] = jnp.zeros((HALO, d), F32)

    cw = V7X_MXU_WIDTH

    def chain(r0):
        rows = slice(r0, r0 + SUB_TILE)
        v = {}

        def pre():
            v["h"] = _rms(x_ref[rows, :], gpre_ref[...]).astype(BF16)

        def chunk(c):
            cols = slice(c * cw, (c + 1) * cw)
            gb = _dot(v["h"], win_ref[:, c * cw:(c + 1) * cw])
            gc = _dot(v["h"], win_ref[:, d + c * cw:d + (c + 1) * cw])
            u = _dot(v["h"], win_ref[:, 2 * d + c * cw:2 * d + (c + 1) * cw])
            conv = _causal_conv_cols(zbuf, gc * u, wconv_ref, cols, r0)
            mbuf[rows, cols] = (gb * conv).astype(BF16)

        def out_proj():
            v["mix"] = _dot(mbuf[rows, :], wout_ref[...])

        def post():
            o_ref[rows, :] = x_ref[rows, :] + _rms(v["mix"], gpost_ref[...])

        return [pre] + [functools.partial(chunk, c) for c in range(d // cw)] + [out_proj, post]

    _emit_staggered([chain(r0) for r0 in range(0, tm, SUB_TILE)])
    zbuf[0:HALO, :] = zbuf[tm:tm + HALO, :]


def _mixer(x2, g_pre, g_post, w_in, w_conv, w_out, seq):
    t, d = x2.shape
    tm = TOKEN_TILE
    kern = functools.partial(_mixer_kernel, tm=tm, tiles_per_seq=seq // tm, d=d)
    row = pl.BlockSpec((tm, d), lambda i: (i, 0))
    return pl.pallas_call(
        kern,
        grid=(t // tm,),
        in_specs=[row, _resident((1, d)), _resident((1, d)), _resident((d, 3 * d)),
                  _resident((CONV_WIDTH, d)), _resident((d, d))],
        out_specs=row,
        out_shape=jax.ShapeDtypeStruct((t, d), F32),
        scratch_shapes=[pltpu.VMEM((tm + HALO, d), F32), pltpu.VMEM((tm, d), BF16)],
        compiler_params=pltpu.CompilerParams(
            dimension_semantics=("arbitrary",), vmem_limit_bytes=V7X_VMEM_LIMIT),
        name="sconv_mixer",
    )(x2, g_pre, g_post, w_in, w_conv, w_out)


def _ffn_ple_kernel(*refs, tm, tiles_per_seq, d, f, with_attn_out):
    if with_attn_out:
        (x_ref, p_ref, a_ref, gains_ref, wo_ref, wup_ref, wconv_ref, wdown_ref, wgate_ref,
         wproj_ref, o_ref, ybuf, abuf) = refs
    else:
        (x_ref, p_ref, gains_ref, wup_ref, wconv_ref, wdown_ref, wgate_ref,
         wproj_ref, o_ref, ybuf, abuf) = refs
    i = pl.program_id(0)

    @pl.when(lax.rem(i, tiles_per_seq) == 0)
    def _():
        ybuf[0:HALO, :] = jnp.zeros((HALO, 2 * f), F32)

    cw = V7X_MXU_WIDTH

    def chain(r0):
        rows = slice(r0, r0 + SUB_TILE)
        v = {}

        def pre():
            x = x_ref[rows, :]
            if with_attn_out:
                x = x + _rms(_dot(a_ref[rows, :], wo_ref[...]), gains_ref[4:5, :])
            v["x"] = x
            v["hn"] = _rms(x, gains_ref[0:1, :]).astype(BF16)

        def chunk(c):
            gcols = slice(c * cw, (c + 1) * cw)
            ucols = slice(f + c * cw, f + (c + 1) * cw)
            gate = _causal_conv_cols(ybuf, _dot(v["hn"], wup_ref[:, c * cw:(c + 1) * cw]),
                                     wconv_ref, gcols, r0)
            up = _causal_conv_cols(ybuf, _dot(v["hn"], wup_ref[:, f + c * cw:f + (c + 1) * cw]),
                                   wconv_ref, ucols, r0)
            abuf[rows, gcols] = (gate * _sigmoid(gate) * up).astype(BF16)

        def down():
            v["ffn"] = _dot(abuf[rows, :], wdown_ref[...])

        def post_ffn():
            v["x"] = v["x"] + _rms(v["ffn"], gains_ref[1:2, :])
            v["hg"] = _rms(v["x"], gains_ref[2:3, :]).astype(BF16)

        def embed():
            v["gate"] = _dot(v["hg"], wgate_ref[...])
            v["emb"] = _dot(p_ref[rows, :].astype(BF16), wproj_ref[...])

        def post_embed():
            o_ref[rows, :] = v["x"] + _rms(_sigmoid(v["gate"]) * v["emb"], gains_ref[3:4, :])

        return ([pre] + [functools.partial(chunk, c) for c in range(f // cw)]
                + [down, post_ffn, embed, post_embed])

    _emit_staggered([chain(r0) for r0 in range(0, tm, SUB_TILE)])
    ybuf[0:HALO, :] = ybuf[tm:tm + HALO, :]


def _ffn_ple(x2, p3, layer, gains, w_up, w_conv, w_down, w_gate, w_proj, seq, attn=None, w_o=None):
    t, d = x2.shape
    f = w_down.shape[1]
    pd = p3.shape[2]
    tm = TOKEN_TILE
    with_attn_out = attn is not None
    kern = functools.partial(_ffn_ple_kernel, tm=tm, tiles_per_seq=seq // tm, d=d, f=f,
                             with_attn_out=with_attn_out)
    row = pl.BlockSpec((tm, d), lambda i: (i, 0))
    prow = pl.BlockSpec((None, tm, pd), lambda i: (layer, i, 0))
    args = [x2, p3]
    specs = [row, prow]
    if with_attn_out:
        args.append(attn)
        specs.append(row)
    args.append(gains)
    specs.append(_resident(gains.shape))
    if with_attn_out:
        args.append(w_o)
        specs.append(_resident((d, d)))
    args += [w_up, w_conv, w_down, w_gate, w_proj]
    specs += [_resident((d, 2 * f), layer), _resident((CONV_WIDTH, 2 * f), layer), _resident((f, d), layer),
              _resident((d, d), layer), _resident((pd, d), layer)]
    return pl.pallas_call(
        kern,
        grid=(t // tm,),
        in_specs=specs,
        out_specs=row,
        out_shape=jax.ShapeDtypeStruct((t, d), F32),
        scratch_shapes=[pltpu.VMEM((tm + HALO, 2 * f), F32), pltpu.VMEM((tm, f), BF16)],
        compiler_params=pltpu.CompilerParams(
            dimension_semantics=("arbitrary",), vmem_limit_bytes=V7X_VMEM_LIMIT),
        name="attn_out_ffn_ple" if with_attn_out else "ffn_ple",
    )(*args)


def _qkv_kernel(x_ref, gq_ref, gkv_ref, wq_ref, wk_ref, wv_ref, qt_ref, k_ref, vt_ref, wqt_ref, wvt_ref):
    @pl.when((pl.program_id(0) == 0) & (pl.program_id(1) == 0))
    def _():
        wqt_ref[...] = wq_ref[...].T
        wvt_ref[...] = wv_ref[...].T

    def chain(r0):
        rows = slice(r0, r0 + SUB_TILE)
        v = {}

        def pre():
            xn = _normalize(x_ref[0, rows, :])
            v["h"] = (xn * gq_ref[...]).astype(BF16)
            v["hk"] = (xn * gkv_ref[...]).astype(BF16)

        def q_proj():
            qt_ref[0, :, rows] = (_dot_nt(wqt_ref[...], v["h"]) * (HEAD_DIM ** -0.5 * LOG2E)).astype(BF16)

        def k_proj():
            k_ref[0, rows, :] = _dot(v["hk"], wk_ref[...]).astype(BF16)

        def v_proj():
            vt_ref[0, :, rows] = _dot_nt(wvt_ref[...], v["hk"]).astype(BF16)

        return [pre, q_proj, k_proj, v_proj]

    _emit_staggered([chain(r0) for r0 in range(0, x_ref.shape[1], SUB_TILE)])


def _qkv(x3, g_q, g_kv, w_q, w_kv):
    b, s, d = x3.shape
    tm = TOKEN_TILE
    row = pl.BlockSpec((1, tm, d), lambda bi, i: (bi, i, 0))
    col = pl.BlockSpec((1, d, tm), lambda bi, i: (bi, 0, i))
    w_k_spec = pl.BlockSpec((d, d), lambda *_: (0, 0), pipeline_mode=pl.Buffered(1))
    w_v_spec = pl.BlockSpec((d, d), lambda *_: (0, 1), pipeline_mode=pl.Buffered(1))
    return pl.pallas_call(
        _qkv_kernel,
        grid=(b, s // tm),
        in_specs=[row, _resident((1, d)), _resident((1, d)), _resident((d, d)), w_k_spec, w_v_spec],
        out_specs=[col, row, col],
        out_shape=[jax.ShapeDtypeStruct((b, d, s), BF16), jax.ShapeDtypeStruct((b, s, d), BF16),
                   jax.ShapeDtypeStruct((b, d, s), BF16)],
        scratch_shapes=[pltpu.VMEM((d, d), BF16), pltpu.VMEM((d, d), BF16)],
        compiler_params=pltpu.CompilerParams(
            dimension_semantics=("arbitrary", "arbitrary"), vmem_limit_bytes=V7X_VMEM_LIMIT),
        name="qkv_proj",
    )(x3, g_q, g_kv, w_q, w_kv, w_kv)


def _bucket_thresholds(seq):
    dist = np.arange(seq)
    max_exact = N_BUCKETS // 2
    large = max_exact + (np.log(np.maximum(dist, 1).astype(np.float32) / np.float32(max_exact))
                         / np.float32(math.log(MAX_DISTANCE / max_exact))
                         * np.float32(N_BUCKETS - max_exact)).astype(np.int32)
    bucket = np.where(dist < max_exact, dist, np.minimum(large, N_BUCKETS - 1))
    assert np.all(np.diff(bucket) >= 0)
    thr = [int(np.argmax(bucket >= b)) if np.any(bucket >= b) else seq for b in range(N_BUCKETS)]
    assert np.all(bucket[MAX_DISTANCE:] == N_BUCKETS - 1)
    return thr


def _attn_pairs(nq):
    pairs = [(qi, qi - n, min(n, 2)) for qi in range(nq) for n in range(qi + 1)]
    pairs.append(pairs[-1])
    return np.asarray(pairs, np.int32).T


def _attn_kernel(rb_ref, pairs_ref, qt_ref, k_ref, vt_ref, lam_ref, g_ref, o_ref,
                 bias_sc, q_sc, v_sc, s_sc, mx_sc, m_sc, l_sc, acc_sc,
                 *, t, nq, thresholds, lambda_init):
    h = pl.program_id(0)
    bi = pl.program_id(1)

    @pl.when(bi == 0)
    def _():
        row = lax.broadcasted_iota(jnp.int32, (t, t), 0)
        col = lax.broadcasted_iota(jnp.int32, (t, t), 1)
        for tile, delta in ((0, 0), (1, t)):
            dist = col - row + delta
            val = jnp.full((t, t), rb_ref[0, h], F32)
            for b in range(1, N_BUCKETS):
                val = jnp.where(dist >= thresholds[b], rb_ref[b, h], val)
            val = val * LOG2E
            if tile == 0:
                val = jnp.where(dist >= 0, val, -jnp.inf)
            bias_sc[tile] = val
        bias_sc[2] = jnp.full((t, t), rb_ref[N_BUCKETS - 1, h] * LOG2E, F32)
        zero = jnp.zeros((HEAD_DIM, nq * t), BF16)
        q_sc[0, HEAD_DIM:, :] = zero
        q_sc[1, 0:HEAD_DIM, :] = zero
        v_sc[V_HEAD_DIM:, :] = jnp.ones((ONES_ROWS, nq * t), BF16)
        l_sc[...] = jnp.zeros(l_sc.shape, F32)
        acc_sc[...] = jnp.zeros(acc_sc.shape, F32)

    q_sc[0, 0:HEAD_DIM, :] = qt_ref[0, 0:HEAD_DIM, :]
    q_sc[1, HEAD_DIM:, :] = qt_ref[0, HEAD_DIM:, :]
    v_sc[0:V_HEAD_DIM, :] = vt_ref[0]
    m_sc[...] = jnp.full(m_sc.shape, -jnp.inf, F32)

    def scores(slot, i):
        qi, j, kind = pairs_ref[0, i], pairs_ref[1, i], pairs_ref[2, i]
        kblk = k_ref[0, pl.ds(pl.multiple_of(j * t, t), t), :]
        for mp in range(2):
            s = _dot(kblk, q_sc[mp, :, pl.ds(pl.multiple_of(qi * t, t), t)]) + bias_sc[kind]
            s_sc[slot, mp] = s
            mx_sc[slot, mp] = jnp.max(s, axis=0, keepdims=True)

    def softmax_pv(slot, i):
        qi, j = pairs_ref[0, i], pairs_ref[1, i]
        vblk = v_sc[:, pl.ds(pl.multiple_of(j * t, t), t)]
        for mp in range(2):
            m_old = m_sc[qi, mp]
            m_new = jnp.maximum(m_old, mx_sc[slot, mp])
            alpha = jnp.exp2(m_old - m_new)
            p = jnp.exp2((s_sc[slot, mp] - m_new).astype(BF16))
            pv = _dot(vblk, p)
            l_sc[qi, mp] = alpha * l_sc[qi, mp] + pv[V_HEAD_DIM:V_HEAD_DIM + 1]
            acc_sc[qi, mp] = alpha * acc_sc[qi, mp] + pv[0:V_HEAD_DIM]
            m_sc[qi, mp] = m_new

    n_pairs = pairs_ref.shape[1] - 1
    scores(0, 0)

    def stages(first, count):
        for u in range(count):
            scores((u + 1) % 2, first + u + 1)
            softmax_pv(u % 2, first + u)

    def trip_body(trip, carry):
        stages(ATTN_STAGES_PER_TRIP * trip, ATTN_STAGES_PER_TRIP)
        return carry

    lax.fori_loop(0, n_pairs // ATTN_STAGES_PER_TRIP, trip_body, 0)
    done = n_pairs - n_pairs % ATTN_STAGES_PER_TRIP
    stages(done, n_pairs - done)

    lv = lam_ref[...]
    lam = (jnp.exp(jnp.sum(lv[0:1] * lv[1:2], axis=-1, keepdims=True))
           - jnp.exp(jnp.sum(lv[2:3] * lv[3:4], axis=-1, keepdims=True)) + lambda_init)
    gain = g_ref[...] * (1.0 - lambda_init)
    for qi in range(nq):
        o = acc_sc[qi, 0] * (1.0 / l_sc[qi, 0]) - acc_sc[qi, 1] * (lam / l_sc[qi, 1])
        o = o * lax.rsqrt(jnp.mean(o * o, axis=0, keepdims=True) + EPS) * gain
        o_ref[0, qi * t:(qi + 1) * t, :] = o.T.astype(BF16)


def _attention(qt, k, vt, rel_bias, lam_vecs, g_subln, lambda_init):
    b, s, _ = k.shape
    t = ATTN_TILE
    assert s % t == 0 and t >= MAX_DISTANCE
    nq = s // t
    hd = V_HEAD_DIM
    pairs = _attn_pairs(nq)
    kern = functools.partial(_attn_kernel, t=t, nq=nq, thresholds=_bucket_thresholds(s),
                             lambda_init=lambda_init)
    return pl.pallas_call(
        kern,
        grid=(N_HEADS, b),
        in_specs=[
            pl.BlockSpec(memory_space=pltpu.SMEM),
            pl.BlockSpec(memory_space=pltpu.SMEM),
            pl.BlockSpec((1, hd, s), lambda h, bi: (bi, h, 0)),
            pl.BlockSpec((1, s, hd), lambda h, bi: (bi, 0, h)),
            pl.BlockSpec((1, hd, s), lambda h, bi: (bi, h, 0)),
            _resident(lam_vecs.shape),
            _resident((hd, 1)),
        ],
        out_specs=pl.BlockSpec((1, s, hd), lambda h, bi: (bi, 0, h)),
        out_shape=jax.ShapeDtypeStruct((b, s, N_HEADS * hd), BF16),
        scratch_shapes=[
            pltpu.VMEM((3, t, t), F32),
            pltpu.VMEM((2, hd, s), BF16),
            pltpu.VMEM((hd + ONES_ROWS, s), BF16),
            pltpu.VMEM((2, 2, t, t), F32),
            pltpu.VMEM((2, 2, 1, t), F32),
            pltpu.VMEM((nq, 2, 1, t), F32),
            pltpu.VMEM((nq, 2, 1, t), F32),
            pltpu.VMEM((nq, 2, hd, t), F32),
        ],
        compiler_params=pltpu.CompilerParams(
            dimension_semantics=("arbitrary", "arbitrary"),
            vmem_limit_bytes=V7X_VMEM_LIMIT),
        name="diff_attention",
    )(rel_bias, jnp.asarray(pairs), qt, k, vt, lam_vecs, g_subln)


def kernel(x, p, g_pre_mix, g_post_mix, g_pre_ffn, g_post_ffn, g_pre_ple, g_post_ple,
           w_sc_in, w_sc_conv, w_sc_out, g_kv, w_kv, rel_bias, w_q, diff_lambda,
           g_subln, w_o, w_ffn_up, w_ffn_conv, w_ffn_down, w_ple_gate, w_ple_proj):
    b, s, d = x.shape
    depth = p.shape[0]
    n_a = depth // 2
    qk_width = N_HEADS * 2 * HEAD_DIM
    x2 = x.reshape(b * s, d)
    p2 = p.reshape(depth, b * s, p.shape[-1])
    bf = lambda w: w.astype(BF16)

    for i in range(depth):
        attn = None
        if i < n_a:
            x2 = _mixer(x2, g_pre_mix[i][None], g_post_mix[i][None], bf(w_sc_in[i]),
                        w_sc_conv[i], bf(w_sc_out[i]), s)
        else:
            j = i - n_a
            assert j == 0, "one attention layer: K/V and Q read the same stream"
            assert w_kv.shape == (d, 2 * qk_width) and qk_width == d
            lambda_init = 0.8 - 0.6 * math.exp(-0.3 * i)
            qt, k, vt = _qkv(x2.reshape(b, s, d), g_pre_mix[i][None], g_kv[None], bf(w_q[j]), bf(w_kv))
            attn = _attention(qt, k, vt, rel_bias, diff_lambda[j], g_subln[j][:, None],
                              lambda_init).reshape(b * s, d)
        gains = jnp.stack([g_pre_ffn[i], g_post_ffn[i], g_pre_ple[i], g_post_ple[i], g_post_mix[i]])
        x2 = _ffn_ple(x2, p2, i, gains, bf(w_ffn_up), w_ffn_conv, bf(w_ffn_down),
                      bf(w_ple_gate), bf(w_ple_proj), s,
                      attn=attn, w_o=None if attn is None else bf(w_o[i - n_a]))
    return x2.reshape(b, s, d)
```

```python
import functools
import math

import numpy as np
import jax
import jax.numpy as jnp
from jax import lax
from jax.experimental import pallas as pl
from jax.experimental.pallas import tpu as pltpu

F32 = jnp.float32
BF16 = jnp.bfloat16

EPS = 1e-6
LOG2E = math.log2(math.e)
CONV_WIDTH = 3
N_HEADS = 8
HEAD_DIM = 64
V_HEAD_DIM = 2 * HEAD_DIM
N_BUCKETS = 32
MAX_DISTANCE = 128

V7X_SUBLANES = 8
V7X_MXU_WIDTH = 256
V7X_VMEM_LIMIT = 56 * 1024 * 1024

TOKEN_TILE = 512
SUB_TILE = 256
ATTN_TILE = 512
ATTN_STAGES_PER_TRIP = 12
HALO = V7X_SUBLANES
ONES_ROWS = 2 * V7X_SUBLANES


def _dot(a, b):
    return jnp.dot(a, b, preferred_element_type=F32)


def _dot_nt(a, b):
    return lax.dot_general(a, b, (((1,), (1,)), ((), ())), preferred_element_type=F32)


def _normalize(x):
    return x * lax.rsqrt(jnp.mean(x * x, axis=-1, keepdims=True) + EPS)


def _rms(x, g):
    return _normalize(x) * g


def _sigmoid(x):
    return 1.0 / (1.0 + jnp.exp(-x))


def _causal_conv_cols(buf, y, wconv_ref, cols, r0):
    rows = y.shape[0]
    buf[HALO + r0:HALO + r0 + rows, cols] = y
    return (wconv_ref[0:1, cols] * buf[HALO + r0 - 2:HALO + r0 - 2 + rows, cols]
            + wconv_ref[1:2, cols] * buf[HALO + r0 - 1:HALO + r0 - 1 + rows, cols]
            + wconv_ref[2:3, cols] * y)


def _emit_staggered(chains):
    depth = max(len(chain) for chain in chains)
    for step in range(depth + len(chains) - 1):
        for lag, chain in enumerate(chains):
            if 0 <= step - lag < len(chain):
                chain[step - lag]()


def _resident(shape, layer=None):
    if layer is None:
        return pl.BlockSpec(shape, lambda *_: (0,) * len(shape), pipeline_mode=pl.Buffered(1))
    return pl.BlockSpec((None,) + tuple(shape), lambda *_: (layer,) + (0,) * len(shape),
                        pipeline_mode=pl.Buffered(1))


def _mixer_kernel(x_ref, gpre_ref, gpost_ref, win_ref, wconv_ref, wout_ref, o_ref,
                  zbuf, mbuf, *, tm, tiles_per_seq, d):
    i = pl.program_id(0)

    @pl.when(lax.rem(i, tiles_per_seq) == 0)
    def _():
        zbuf[0:HALO, :] = jnp.zeros((HALO, d), F32)

    cw = V7X_MXU_WIDTH

    def chain(r0):
        rows = slice(r0, r0 + SUB_TILE)
        v = {}

        def pre():
            v["h"] = _rms(x_ref[rows, :], gpre_ref[...]).astype(BF16)

        def chunk(c):
            cols = slice(c * cw, (c + 1) * cw)
            gb = _dot(v["h"], win_ref[:, c * cw:(c + 1) * cw])
            gc = _dot(v["h"], win_ref[:, d + c * cw:d + (c + 1) * cw])
            u = _dot(v["h"], win_ref[:, 2 * d + c * cw:2 * d + (c + 1) * cw])
            conv = _causal_conv_cols(zbuf, gc * u, wconv_ref, cols, r0)
            mbuf[rows, cols] = (gb * conv).astype(BF16)

        def out_proj():
            v["mix"] = _dot(mbuf[rows, :], wout_ref[...])

        def post():
            o_ref[rows, :] = x_ref[rows, :] + _rms(v["mix"], gpost_ref[...])

        return [pre] + [functools.partial(chunk, c) for c in range(d // cw)] + [out_proj, post]

    _emit_staggered([chain(r0) for r0 in range(0, tm, SUB_TILE)])
    zbuf[0:HALO, :] = zbuf[tm:tm + HALO, :]


def _mixer(x2, g_pre, g_post, w_in, w_conv, w_out, seq):
    t, d = x2.shape
    tm = TOKEN_TILE
    kern = functools.partial(_mixer_kernel, tm=tm, tiles_per_seq=seq // tm, d=d)
    row = pl.BlockSpec((tm, d), lambda i: (i, 0))
    return pl.pallas_call(
        kern,
        grid=(t // tm,),
        in_specs=[row, _resident((1, d)), _resident((1, d)), _resident((d, 3 * d)),
                  _resident((CONV_WIDTH, d)), _resident((d, d))],
        out_specs=row,
        out_shape=jax.ShapeDtypeStruct((t, d), F32),
        scratch_shapes=[pltpu.VMEM((tm + HALO, d), F32), pltpu.VMEM((tm, d), BF16)],
        compiler_params=pltpu.CompilerParams(
            dimension_semantics=("arbitrary",), vmem_limit_bytes=V7X_VMEM_LIMIT),
        name="sconv_mixer",
    )(x2, g_pre, g_post, w_in, w_conv, w_out)


def _ffn_ple_kernel(*refs, tm, tiles_per_seq, d, f, with_attn_out):
    if with_attn_out:
        (x_ref, p_ref, a_ref, gains_ref, wo_ref, wup_ref, wconv_ref, wdown_ref, wgate_ref,
         wproj_ref, o_ref, ybuf, abuf) = refs
    else:
        (x_ref, p_ref, gains_ref, wup_ref, wconv_ref, wdown_ref, wgate_ref,
         wproj_ref, o_ref, ybuf, abuf) = refs
    i = pl.program_id(0)

    @pl.when(lax.rem(i, tiles_per_seq) == 0)
    def _():
        ybuf[0:HALO, :] = jnp.zeros((HALO, 2 * f), F32)

    cw = V7X_MXU_WIDTH

    def chain(r0):
        rows = slice(r0, r0 + SUB_TILE)
        v = {}

        def pre():
            x = x_ref[rows, :]
            if with_attn_out:
                x = x + _rms(_dot(a_ref[rows, :], wo_ref[...]), gains_ref[4:5, :])
            v["x"] = x
            v["hn"] = _rms(x, gains_ref[0:1, :]).astype(BF16)

        def chunk(c):
            gcols = slice(c * cw, (c + 1) * cw)
            ucols = slice(f + c * cw, f + (c + 1) * cw)
            gate = _causal_conv_cols(ybuf, _dot(v["hn"], wup_ref[:, c * cw:(c + 1) * cw]),
                                     wconv_ref, gcols, r0)
            up = _causal_conv_cols(ybuf, _dot(v["hn"], wup_ref[:, f + c * cw:f + (c + 1) * cw]),
                                   wconv_ref, ucols, r0)
            abuf[rows, gcols] = (gate * _sigmoid(gate) * up).astype(BF16)

        def down():
            v["ffn"] = _dot(abuf[rows, :], wdown_ref[...])

        def post_ffn():
            v["x"] = v["x"] + _rms(v["ffn"], gains_ref[1:2, :])
            v["hg"] = _rms(v["x"], gains_ref[2:3, :]).astype(BF16)

        def embed():
            v["gate"] = _dot(v["hg"], wgate_ref[...])
            v["emb"] = _dot(p_ref[rows, :].astype(BF16), wproj_ref[...])

        def post_embed():
            o_ref[rows, :] = v["x"] + _rms(_sigmoid(v["gate"]) * v["emb"], gains_ref[3:4, :])

        return ([pre] + [functools.partial(chunk, c) for c in range(f // cw)]
                + [down, post_ffn, embed, post_embed])

    _emit_staggered([chain(r0) for r0 in range(0, tm, SUB_TILE)])
    ybuf[0:HALO, :] = ybuf[tm:tm + HALO, :]


def _ffn_ple(x2, p3, layer, gains, w_up, w_conv, w_down, w_gate, w_proj, seq, attn=None, w_o=None):
    t, d = x2.shape
    f = w_down.shape[1]
    pd = p3.shape[2]
    tm = TOKEN_TILE
    with_attn_out = attn is not None
    kern = functools.partial(_ffn_ple_kernel, tm=tm, tiles_per_seq=seq // tm, d=d, f=f,
                             with_attn_out=with_attn_out)
    row = pl.BlockSpec((tm, d), lambda i: (i, 0))
    prow = pl.BlockSpec((None, tm, pd), lambda i: (layer, i, 0))
    args = [x2, p3]
    specs = [row, prow]
    if with_attn_out:
        args.append(attn)
        specs.append(row)
    args.append(gains)
    specs.append(_resident(gains.shape))
    if with_attn_out:
        args.append(w_o)
        specs.append(_resident((d, d)))
    args += [w_up, w_conv, w_down, w_gate, w_proj]
    specs += [_resident((d, 2 * f), layer), _resident((CONV_WIDTH, 2 * f), layer), _resident((f, d), layer),
              _resident((d, d), layer), _resident((pd, d), layer)]
    return pl.pallas_call(
        kern,
        grid=(t // tm,),
        in_specs=specs,
        out_specs=row,
        out_shape=jax.ShapeDtypeStruct((t, d), F32),
        scratch_shapes=[pltpu.VMEM((tm + HALO, 2 * f), F32), pltpu.VMEM((tm, f), BF16)],
        compiler_params=pltpu.CompilerParams(
            dimension_semantics=("arbitrary",), vmem_limit_bytes=V7X_VMEM_LIMIT),
        name="attn_out_ffn_ple" if with_attn_out else "ffn_ple",
    )(*args)


def _qkv_kernel(x_ref, gq_ref, gkv_ref, wq_ref, wk_ref, wv_ref, qt_ref, k_ref, vt_ref, wqt_ref, wvt_ref):
    @pl.when((pl.program_id(0) == 0) & (pl.program_id(1) == 0))
    def _():
        wqt_ref[...] = wq_ref[...].T
        wvt_ref[...] = wv_ref[...].T

    def chain(r0):
        rows = slice(r0, r0 + SUB_TILE)
        v = {}

        def pre():
            xn = _normalize(x_ref[0, rows, :])
            v["h"] = (xn * gq_ref[...]).astype(BF16)
            v["hk"] = (xn * gkv_ref[...]).astype(BF16)

        def q_proj():
            qt_ref[0, :, rows] = (_dot_nt(wqt_ref[...], v["h"]) * (HEAD_DIM ** -0.5 * LOG2E)).astype(BF16)

        def k_proj():
            k_ref[0, rows, :] = _dot(v["hk"], wk_ref[...]).astype(BF16)

        def v_proj():
            vt_ref[0, :, rows] = _dot_nt(wvt_ref[...], v["hk"]).astype(BF16)

        return [pre, q_proj, k_proj, v_proj]

    _emit_staggered([chain(r0) for r0 in range(0, x_ref.shape[1], SUB_TILE)])


def _qkv(x3, g_q, g_kv, w_q, w_kv):
    b, s, d = x3.shape
    tm = TOKEN_TILE
    row = pl.BlockSpec((1, tm, d), lambda bi, i: (bi, i, 0))
    col = pl.BlockSpec((1, d, tm), lambda bi, i: (bi, 0, i))
    w_k_spec = pl.BlockSpec((d, d), lambda *_: (0, 0), pipeline_mode=pl.Buffered(1))
    w_v_spec = pl.BlockSpec((d, d), lambda *_: (0, 1), pipeline_mode=pl.Buffered(1))
    return pl.pallas_call(
        _qkv_kernel,
        grid=(b, s // tm),
        in_specs=[row, _resident((1, d)), _resident((1, d)), _resident((d, d)), w_k_spec, w_v_spec],
        out_specs=[col, row, col],
        out_shape=[jax.ShapeDtypeStruct((b, d, s), BF16), jax.ShapeDtypeStruct((b, s, d), BF16),
                   jax.ShapeDtypeStruct((b, d, s), BF16)],
        scratch_shapes=[pltpu.VMEM((d, d), BF16), pltpu.VMEM((d, d), BF16)],
        compiler_params=pltpu.CompilerParams(
            dimension_semantics=("arbitrary", "arbitrary"), vmem_limit_bytes=V7X_VMEM_LIMIT),
        name="qkv_proj",
    )(x3, g_q, g_kv, w_q, w_kv, w_kv)


def _bucket_thresholds(seq):
    dist = np.arange(seq)
    max_exact = N_BUCKETS // 2
    large = max_exact + (np.log(np.maximum(dist, 1).astype(np.float32) / np.float32(max_exact))
                         / np.float32(math.log(MAX_DISTANCE / max_exact))
                         * np.float32(N_BUCKETS - max_exact)).astype(np.int32)
    bucket = np.where(dist < max_exact, dist, np.minimum(large, N_BUCKETS - 1))
    assert np.all(np.diff(bucket) >= 0)
    thr = [int(np.argmax(bucket >= b)) if np.any(bucket >= b) else seq for b in range(N_BUCKETS)]
    assert np.all(bucket[MAX_DISTANCE:] == N_BUCKETS - 1)
    return thr


def _attn_pairs(nq):
    pairs = [(qi, qi - n, min(n, 2)) for qi in range(nq) for n in range(qi + 1)]
    pairs.append(pairs[-1])
    return np.asarray(pairs, np.int32).T


def _attn_kernel(rb_ref, pairs_ref, qt_ref, k_ref, vt_ref, lam_ref, g_ref, o_ref,
                 bias_sc, q_sc, v_sc, s_sc, mx_sc, m_sc, l_sc, acc_sc,
                 *, t, nq, thresholds, lambda_init):
    h = pl.program_id(0)
    bi = pl.program_id(1)

    @pl.when(bi == 0)
    def _():
        row = lax.broadcasted_iota(jnp.int32, (t, t), 0)
        col = lax.broadcasted_iota(jnp.int32, (t, t), 1)
        for tile, delta in ((0, 0), (1, t)):
            dist = col - row + delta
            val = jnp.full((t, t), rb_ref[0, h], F32)
            for b in range(1, N_BUCKETS):
                val = jnp.where(dist >= thresholds[b], rb_ref[b, h], val)
            val = val * LOG2E
            if tile == 0:
                val = jnp.where(dist >= 0, val, -jnp.inf)
            bias_sc[tile] = val
        bias_sc[2] = jnp.full((t, t), rb_ref[N_BUCKETS - 1, h] * LOG2E, F32)
        zero = jnp.zeros((HEAD_DIM, nq * t), BF16)
        q_sc[0, HEAD_DIM:, :] = zero
        q_sc[1, 0:HEAD_DIM, :] = zero
        v_sc[V_HEAD_DIM:, :] = jnp.ones((ONES_ROWS, nq * t), BF16)
        l_sc[...] = jnp.zeros(l_sc.shape, F32)
        acc_sc[...] = jnp.zeros(acc_sc.shape, F32)

    q_sc[0, 0:HEAD_DIM, :] = qt_ref[0, 0:HEAD_DIM, :]
    q_sc[1, HEAD_DIM:, :] = qt_ref[0, HEAD_DIM:, :]
    v_sc[0:V_HEAD_DIM, :] = vt_ref[0]
    m_sc[...] = jnp.full(m_sc.shape, -jnp.inf, F32)

    cw = V7X_MXU_WIDTH

    def scores(slot, i, mp, c0):
        qi, j, kind = pairs_ref[0, i], pairs_ref[1, i], pairs_ref[2, i]
        cols = slice(c0, c0 + cw)
        kblk = k_ref[0, pl.ds(pl.multiple_of(j * t, t), t), :]
        qblk = q_sc[mp, :, pl.ds(pl.multiple_of(qi * t + c0, cw), cw)]
        s = _dot(kblk, qblk) + bias_sc[kind, :, cols]
        s_sc[slot, mp, :, cols] = s
        mx_sc[slot, mp, :, cols] = jnp.max(s, axis=0, keepdims=True)

    def softmax_pv(slot, i, mp, c0):
        qi, j = pairs_ref[0, i], pairs_ref[1, i]
        cols = slice(c0, c0 + cw)
        vblk = v_sc[:, pl.ds(pl.multiple_of(j * t, t), t)]
        m_old = m_sc[qi, mp, :, cols]
        m_new = jnp.maximum(m_old, mx_sc[slot, mp, :, cols])
        alpha = jnp.exp2(m_old - m_new)
        p = jnp.exp2((s_sc[slot, mp, :, cols] - m_new).astype(BF16))
        pv = _dot(vblk, p)
        l_sc[qi, mp, :, cols] = alpha * l_sc[qi, mp, :, cols] + pv[V_HEAD_DIM:V_HEAD_DIM + 1]
        acc_sc[qi, mp, :, cols] = alpha * acc_sc[qi, mp, :, cols] + pv[0:V_HEAD_DIM]
        m_sc[qi, mp, :, cols] = m_new

    n_pairs = pairs_ref.shape[1] - 1
    pieces = [(mp, c0) for mp in range(2) for c0 in range(0, t, cw)]
    for mp, c0 in pieces:
        scores(0, 0, mp, c0)

    def stages(first, count):
        for u in range(count):
            for mp, c0 in pieces:
                scores((u + 1) % 2, first + u + 1, mp, c0)
                softmax_pv(u % 2, first + u, mp, c0)

    def trip_body(trip, carry):
        stages(ATTN_STAGES_PER_TRIP * trip, ATTN_STAGES_PER_TRIP)
        return carry

    lax.fori_loop(0, n_pairs // ATTN_STAGES_PER_TRIP, trip_body, 0)
    done = n_pairs - n_pairs % ATTN_STAGES_PER_TRIP
    stages(done, n_pairs - done)

    lv = lam_ref[...]
    lam = (jnp.exp(jnp.sum(lv[0:1] * lv[1:2], axis=-1, keepdims=True))
           - jnp.exp(jnp.sum(lv[2:3] * lv[3:4], axis=-1, keepdims=True)) + lambda_init)
    gain = g_ref[...] * (1.0 - lambda_init)
    for qi in range(nq):
        o = acc_sc[qi, 0] * (1.0 / l_sc[qi, 0]) - acc_sc[qi, 1] * (lam / l_sc[qi, 1])
        o = o * lax.rsqrt(jnp.mean(o * o, axis=0, keepdims=True) + EPS) * gain
        o_ref[0, qi * t:(qi + 1) * t, :] = o.T.astype(BF16)


def _attention(qt, k, vt, rel_bias, lam_vecs, g_subln, lambda_init):
    b, s, _ = k.shape
    t = ATTN_TILE
    assert s % t == 0 and t >= MAX_DISTANCE
    nq = s // t
    hd = V_HEAD_DIM
    pairs = _attn_pairs(nq)
    kern = functools.partial(_attn_kernel, t=t, nq=nq, thresholds=_bucket_thresholds(s),
                             lambda_init=lambda_init)
    return pl.pallas_call(
        kern,
        grid=(N_HEADS, b),
        in_specs=[
            pl.BlockSpec(memory_space=pltpu.SMEM),
            pl.BlockSpec(memory_space=pltpu.SMEM),
            pl.BlockSpec((1, hd, s), lambda h, bi: (bi, h, 0)),
            pl.BlockSpec((1, s, hd), lambda h, bi: (bi, 0, h)),
            pl.BlockSpec((1, hd, s), lambda h, bi: (bi, h, 0)),
            _resident(lam_vecs.shape),
            _resident((hd, 1)),
        ],
        out_specs=pl.BlockSpec((1, s, hd), lambda h, bi: (bi, 0, h)),
        out_shape=jax.ShapeDtypeStruct((b, s, N_HEADS * hd), BF16),
        scratch_shapes=[
            pltpu.VMEM((3, t, t), F32),
            pltpu.VMEM((2, hd, s), BF16),
            pltpu.VMEM((hd + ONES_ROWS, s), BF16),
            pltpu.VMEM((2, 2, t, t), F32),
            pltpu.VMEM((2, 2, 1, t), F32),
            pltpu.VMEM((nq, 2, 1, t), F32),
            pltpu.VMEM((nq, 2, 1, t), F32),
            pltpu.VMEM((nq, 2, hd, t), F32),
        ],
        compiler_params=pltpu.CompilerParams(
            dimension_semantics=("arbitrary", "arbitrary"),
            vmem_limit_bytes=V7X_VMEM_LIMIT),
        name="diff_attention",
    )(rel_bias, jnp.asarray(pairs), qt, k, vt, lam_vecs, g_subln)


def kernel(x, p, g_pre_mix, g_post_mix, g_pre_ffn, g_post_ffn, g_pre_ple, g_post_ple,
           w_sc_in, w_sc_conv, w_sc_out, g_kv, w_kv, rel_bias, w_q, diff_lambda,
           g_subln, w_o, w_ffn_up, w_ffn_conv, w_ffn_down, w_ple_gate, w_ple_proj):
    b, s, d = x.shape
    depth = p.shape[0]
    n_a = depth // 2
    qk_width = N_HEADS * 2 * HEAD_DIM
    x2 = x.reshape(b * s, d)
    p2 = p.reshape(depth, b * s, p.shape[-1])
    bf = lambda w: w.astype(BF16)

    for i in range(depth):
        attn = None
        if i < n_a:
            x2 = _mixer(x2, g_pre_mix[i][None], g_post_mix[i][None], bf(w_sc_in[i]),
                        w_sc_conv[i], bf(w_sc_out[i]), s)
        else:
            j = i - n_a
            assert j == 0, "one attention layer: K/V and Q read the same stream"
            assert w_kv.shape == (d, 2 * qk_width) and qk_width == d
            lambda_init = 0.8 - 0.6 * math.exp(-0.3 * i)
            qt, k, vt = _qkv(x2.reshape(b, s, d), g_pre_mix[i][None], g_kv[None], bf(w_q[j]), bf(w_kv))
            attn = _attention(qt, k, vt, rel_bias, diff_lambda[j], g_subln[j][:, None],
                              lambda_init).reshape(b * s, d)
        gains = jnp.stack([g_pre_ffn[i], g_post_ffn[i], g_pre_ple[i], g_post_ple[i], g_post_mix[i]])
        x2 = _ffn_ple(x2, p2, i, gains, bf(w_ffn_up), w_ffn_conv, bf(w_ffn_down),
                      bf(w_ple_gate), bf(w_ple_proj), s,
                      attn=attn, w_o=None if attn is None else bf(w_o[i - n_a]))
    return x2.reshape(b, s, d)
```

```python
import functools
import math

import numpy as np
import jax
import jax.numpy as jnp
from jax import lax
from jax.experimental import pallas as pl
from jax.experimental.pallas import tpu as pltpu

F32 = jnp.float32
BF16 = jnp.bfloat16

EPS = 1e-6
LOG2E = math.log2(math.e)
CONV_WIDTH = 3
N_HEADS = 8
HEAD_DIM = 64
V_HEAD_DIM = 2 * HEAD_DIM
N_BUCKETS = 32
MAX_DISTANCE = 128

V7X_SUBLANES = 8
V7X_MXU_WIDTH = 256
V7X_VMEM_LIMIT = 56 * 1024 * 1024

TOKEN_TILE = 512
SUB_TILE = 256
ATTN_TILE = 512
ATTN_STAGES_PER_TRIP = 18
HALO = V7X_SUBLANES
BF16_ROWS = 2 * V7X_SUBLANES
ONES_ROWS = BF16_ROWS


def _dot(a, b):
    return jnp.dot(a, b, preferred_element_type=F32)


def _dot_nt(a, b):
    return lax.dot_general(a, b, (((1,), (1,)), ((), ())), preferred_element_type=F32)


def _normalize(x):
    return x * lax.rsqrt(jnp.mean(x * x, axis=-1, keepdims=True) + EPS)


def _rms(x, g):
    return _normalize(x) * g


def _sigmoid(x):
    return 1.0 / (1.0 + jnp.exp2(x * -LOG2E))


def _causal_conv_cols(buf, y, wconv_ref, cols, r0):
    rows = y.shape[0]
    buf[HALO + r0:HALO + r0 + rows, cols] = y
    return (wconv_ref[0:1, cols] * buf[HALO + r0 - 2:HALO + r0 - 2 + rows, cols]
            + wconv_ref[1:2, cols] * buf[HALO + r0 - 1:HALO + r0 - 1 + rows, cols]
            + wconv_ref[2:3, cols] * y)


def _emit_staggered(chains):
    depth = max(len(chain) for chain in chains)
    for step in range(depth + len(chains) - 1):
        for lag, chain in enumerate(chains):
            if 0 <= step - lag < len(chain):
                chain[step - lag]()


def _resident(shape, layer=None):
    if layer is None:
        return pl.BlockSpec(shape, lambda *_: (0,) * len(shape), pipeline_mode=pl.Buffered(1))
    return pl.BlockSpec((None,) + tuple(shape), lambda *_: (layer,) + (0,) * len(shape),
                        pipeline_mode=pl.Buffered(1))


def _cast_plan(w, n_steps):
    rows = w.shape[1]
    r = BF16_ROWS * pl.cdiv(pl.cdiv(rows, n_steps), BF16_ROWS)
    while rows % r:
        r += BF16_ROWS
    return r, rows // r


def _mixer_kernel(*refs, tm, tiles_per_seq, d, n_cast):
    x_ref, gpre_ref, gpost_ref, win_ref, wconv_ref, wout_ref = refs[:6]
    cast_in = refs[6:6 + n_cast]
    o_ref = refs[6 + n_cast]
    cast_out = refs[7 + n_cast:7 + 2 * n_cast]
    zbuf, mbuf = refs[7 + 2 * n_cast:]
    i = pl.program_id(0)

    @pl.when(lax.rem(i, tiles_per_seq) == 0)
    def _():
        zbuf[0:HALO, :] = jnp.zeros((HALO, d), F32)

    def cast(src, dst):
        dst[...] = src[...].astype(BF16)

    casts = [functools.partial(cast, src, dst) for src, dst in zip(cast_in, cast_out)]

    cw = V7X_MXU_WIDTH

    def chain(r0):
        rows = slice(r0, r0 + SUB_TILE)
        v = {}

        def pre():
            v["h"] = _rms(x_ref[rows, :], gpre_ref[...]).astype(BF16)

        def chunk(c):
            cols = slice(c * cw, (c + 1) * cw)
            gb = _dot(v["h"], win_ref[:, c * cw:(c + 1) * cw])
            gc = _dot(v["h"], win_ref[:, d + c * cw:d + (c + 1) * cw])
            u = _dot(v["h"], win_ref[:, 2 * d + c * cw:2 * d + (c + 1) * cw])
            conv = _causal_conv_cols(zbuf, gc * u, wconv_ref, cols, r0)
            mbuf[rows, cols] = (gb * conv).astype(BF16)

        def out_proj():
            v["mix"] = _dot(mbuf[rows, :], wout_ref[...])

        def post():
            o_ref[rows, :] = x_ref[rows, :] + _rms(v["mix"], gpost_ref[...])

        return [pre] + [functools.partial(chunk, c) for c in range(d // cw)] + [out_proj, post]

    _emit_staggered([chain(r0) for r0 in range(0, tm, SUB_TILE)] + [casts])
    zbuf[0:HALO, :] = zbuf[tm:tm + HALO, :]


def _mixer(x2, g_pre, g_post, w_in, w_conv, w_out, seq, to_cast):
    t, d = x2.shape
    tm = TOKEN_TILE
    n_steps = t // tm
    plans = [_cast_plan(w, n_steps) for w in to_cast]
    kern = functools.partial(_mixer_kernel, tm=tm, tiles_per_seq=seq // tm, d=d, n_cast=len(plans))
    row = pl.BlockSpec((tm, d), lambda i: (i, 0))
    cast_specs = [pl.BlockSpec((w.shape[0], r, w.shape[2]), lambda i, nb=nb: (0, jnp.minimum(i, nb - 1), 0))
                  for w, (r, nb) in zip(to_cast, plans)]
    outs = pl.pallas_call(
        kern,
        grid=(n_steps,),
        in_specs=[row, _resident((1, d)), _resident((1, d)), _resident((d, 3 * d)),
                  _resident((CONV_WIDTH, d)), _resident((d, d))] + cast_specs,
        out_specs=[row] + cast_specs,
        out_shape=[jax.ShapeDtypeStruct((t, d), F32)] + [jax.ShapeDtypeStruct(w.shape, BF16) for w in to_cast],
        scratch_shapes=[pltpu.VMEM((tm + HALO, d), F32), pltpu.VMEM((tm, d), BF16)],
        compiler_params=pltpu.CompilerParams(
            dimension_semantics=("arbitrary",), vmem_limit_bytes=V7X_VMEM_LIMIT),
        name="sconv_mixer",
    )(x2, g_pre, g_post, w_in, w_conv, w_out, *to_cast)
    return outs[0], outs[1:]


def _ffn_ple_kernel(*refs, tm, tiles_per_seq, d, f, with_attn_out):
    if with_attn_out:
        (x_ref, p_ref, a_ref, gains_ref, wo_ref, wup_ref, wconv_ref, wdown_ref, wgate_ref,
         wproj_ref, o_ref, ybuf, abuf) = refs
    else:
        (x_ref, p_ref, gains_ref, wup_ref, wconv_ref, wdown_ref, wgate_ref,
         wproj_ref, o_ref, ybuf, abuf) = refs
    i = pl.program_id(0)

    @pl.when(lax.rem(i, tiles_per_seq) == 0)
    def _():
        ybuf[0:HALO, :] = jnp.zeros((HALO, 2 * f), F32)

    cw = V7X_MXU_WIDTH

    def chain(r0):
        rows = slice(r0, r0 + SUB_TILE)
        v = {}

        def pre():
            x = x_ref[rows, :]
            if with_attn_out:
                x = x + _rms(_dot(a_ref[rows, :], wo_ref[...]), gains_ref[4:5, :])
            v["x"] = x
            v["hn"] = _rms(x, gains_ref[0:1, :]).astype(BF16)

        def chunk(c):
            gcols = slice(c * cw, (c + 1) * cw)
            ucols = slice(f + c * cw, f + (c + 1) * cw)
            gate = _causal_conv_cols(ybuf, _dot(v["hn"], wup_ref[:, c * cw:(c + 1) * cw]),
                                     wconv_ref, gcols, r0)
            up = _causal_conv_cols(ybuf, _dot(v["hn"], wup_ref[:, f + c * cw:f + (c + 1) * cw]),
                                   wconv_ref, ucols, r0)
            abuf[rows, gcols] = (gate * _sigmoid(gate) * up).astype(BF16)

        def down():
            v["ffn"] = _dot(abuf[rows, :], wdown_ref[...])

        def post_ffn():
            v["x"] = v["x"] + _rms(v["ffn"], gains_ref[1:2, :])
            v["hg"] = _rms(v["x"], gains_ref[2:3, :]).astype(BF16)

        def embed():
            v["gate"] = _dot(v["hg"], wgate_ref[...])
            v["emb"] = _dot(p_ref[rows, :].astype(BF16), wproj_ref[...])

        def post_embed():
            o_ref[rows, :] = v["x"] + _rms(_sigmoid(v["gate"]) * v["emb"], gains_ref[3:4, :])

        return ([pre] + [functools.partial(chunk, c) for c in range(f // cw)]
                + [down, post_ffn, embed, post_embed])

    _emit_staggered([chain(r0) for r0 in range(0, tm, SUB_TILE)])
    ybuf[0:HALO, :] = ybuf[tm:tm + HALO, :]


def _ffn_ple(x2, p3, layer, gains, w_up, w_conv, w_down, w_gate, w_proj, seq, attn=None, w_o=None):
    t, d = x2.shape
    f = w_down.shape[1]
    pd = p3.shape[2]
    tm = TOKEN_TILE
    with_attn_out = attn is not None
    kern = functools.partial(_ffn_ple_kernel, tm=tm, tiles_per_seq=seq // tm, d=d, f=f,
                             with_attn_out=with_attn_out)
    row = pl.BlockSpec((tm, d), lambda i: (i, 0))
    prow = pl.BlockSpec((None, tm, pd), lambda i: (layer, i, 0))
    args = [x2, p3]
    specs = [row, prow]
    if with_attn_out:
        args.append(attn)
        specs.append(row)
    args.append(gains)
    specs.append(_resident(gains.shape))
    if with_attn_out:
        args.append(w_o)
        specs.append(_resident((d, d)))
    args += [w_up, w_conv, w_down, w_gate, w_proj]
    specs += [_resident((d, 2 * f), layer), _resident((CONV_WIDTH, 2 * f), layer), _resident((f, d), layer),
              _resident((d, d), layer), _resident((pd, d), layer)]
    return pl.pallas_call(
        kern,
        grid=(t // tm,),
        in_specs=specs,
        out_specs=row,
        out_shape=jax.ShapeDtypeStruct((t, d), F32),
        scratch_shapes=[pltpu.VMEM((tm + HALO, 2 * f), F32), pltpu.VMEM((tm, f), BF16)],
        compiler_params=pltpu.CompilerParams(
            dimension_semantics=("arbitrary",), vmem_limit_bytes=V7X_VMEM_LIMIT),
        name="attn_out_ffn_ple" if with_attn_out else "ffn_ple",
    )(*args)


def _qkv_kernel(x_ref, gq_ref, gkv_ref, wq_ref, wk_ref, wv_ref, qt_ref, k_ref, vt_ref, wqt_ref, wvt_ref):
    @pl.when((pl.program_id(0) == 0) & (pl.program_id(1) == 0))
    def _():
        wqt_ref[...] = wq_ref[...].T
        wvt_ref[...] = wv_ref[...].T

    def chain(r0):
        rows = slice(r0, r0 + SUB_TILE)
        v = {}

        def pre():
            xn = _normalize(x_ref[0, rows, :])
            v["h"] = (xn * gq_ref[...]).astype(BF16)
            v["hk"] = (xn * gkv_ref[...]).astype(BF16)

        def q_proj():
            qt_ref[0, :, rows] = (_dot_nt(wqt_ref[...], v["h"]) * (HEAD_DIM ** -0.5 * LOG2E)).astype(BF16)

        def k_proj():
            k_ref[0, rows, :] = _dot(v["hk"], wk_ref[...]).astype(BF16)

        def v_proj():
            vt_ref[0, :, rows] = _dot_nt(wvt_ref[...], v["hk"]).astype(BF16)

        return [pre, q_proj, k_proj, v_proj]

    _emit_staggered([chain(r0) for r0 in range(0, x_ref.shape[1], SUB_TILE)])


def _qkv(x3, g_q, g_kv, w_q, w_kv):
    b, s, d = x3.shape
    tm = TOKEN_TILE
    row = pl.BlockSpec((1, tm, d), lambda bi, i: (bi, i, 0))
    col = pl.BlockSpec((1, d, tm), lambda bi, i: (bi, 0, i))
    w_k_spec = pl.BlockSpec((d, d), lambda *_: (0, 0), pipeline_mode=pl.Buffered(1))
    w_v_spec = pl.BlockSpec((d, d), lambda *_: (0, 1), pipeline_mode=pl.Buffered(1))
    return pl.pallas_call(
        _qkv_kernel,
        grid=(b, s // tm),
        in_specs=[row, _resident((1, d)), _resident((1, d)), _resident((d, d)), w_k_spec, w_v_spec],
        out_specs=[col, row, col],
        out_shape=[jax.ShapeDtypeStruct((b, d, s), BF16), jax.ShapeDtypeStruct((b, s, d), BF16),
                   jax.ShapeDtypeStruct((b, d, s), BF16)],
        scratch_shapes=[pltpu.VMEM((d, d), BF16), pltpu.VMEM((d, d), BF16)],
        compiler_params=pltpu.CompilerParams(
            dimension_semantics=("arbitrary", "arbitrary"), vmem_limit_bytes=V7X_VMEM_LIMIT),
        name="qkv_proj",
    )(x3, g_q, g_kv, w_q, w_kv, w_kv)


def _bucket_thresholds(seq):
    dist = np.arange(seq)
    max_exact = N_BUCKETS // 2
    large = max_exact + (np.log(np.maximum(dist, 1).astype(np.float32) / np.float32(max_exact))
                         / np.float32(math.log(MAX_DISTANCE / max_exact))
                         * np.float32(N_BUCKETS - max_exact)).astype(np.int32)
    bucket = np.where(dist < max_exact, dist, np.minimum(large, N_BUCKETS - 1))
    assert np.all(np.diff(bucket) >= 0)
    thr = [int(np.argmax(bucket >= b)) if np.any(bucket >= b) else seq for b in range(N_BUCKETS)]
    assert np.all(bucket[MAX_DISTANCE:] == N_BUCKETS - 1)
    return thr


def _attn_pairs(nq):
    pairs = [(qi, qi - n, min(n, 2)) for qi in range(nq) for n in range(qi + 1)]
    pairs.append(pairs[-1])
    return np.asarray(pairs, np.int32).T


def _attn_kernel(rb_ref, pairs_ref, qt_ref, k_ref, vt_ref, lam_ref, g_ref, o_ref,
                 bias_sc, q_sc, v_sc, s_sc, mx_sc, m_sc, l_sc, acc_sc,
                 *, t, nq, thresholds, lambda_init):
    h = pl.program_id(0)
    bi = pl.program_id(1)

    @pl.when(bi == 0)
    def _():
        row = lax.broadcasted_iota(jnp.int32, (t, t), 0)
        col = lax.broadcasted_iota(jnp.int32, (t, t), 1)
        for tile, delta in ((0, 0), (1, t)):
            dist = col - row + delta
            val = jnp.full((t, t), rb_ref[0, h], F32)
            for b in range(1, N_BUCKETS):
                val = jnp.where(dist >= thresholds[b], rb_ref[b, h], val)
            val = val * LOG2E
            if tile == 0:
                val = jnp.where(dist >= 0, val, -jnp.inf)
            bias_sc[tile] = val
        bias_sc[2] = jnp.full((t, t), rb_ref[N_BUCKETS - 1, h] * LOG2E, F32)
        zero = jnp.zeros((HEAD_DIM, nq * t), BF16)
        q_sc[0, HEAD_DIM:, :] = zero
        q_sc[1, 0:HEAD_DIM, :] = zero
        v_sc[V_HEAD_DIM:, :] = jnp.ones((ONES_ROWS, nq * t), BF16)
        l_sc[...] = jnp.zeros(l_sc.shape, F32)
        acc_sc[...] = jnp.zeros(acc_sc.shape, F32)

    q_sc[0, 0:HEAD_DIM, :] = qt_ref[0, 0:HEAD_DIM, :]
    q_sc[1, HEAD_DIM:, :] = qt_ref[0, HEAD_DIM:, :]
    v_sc[0:V_HEAD_DIM, :] = vt_ref[0]
    m_sc[...] = jnp.full(m_sc.shape, -jnp.inf, F32)

    cw = V7X_MXU_WIDTH

    def scores(slot, i, mp, c0):
        qi, j, kind = pairs_ref[0, i], pairs_ref[1, i], pairs_ref[2, i]
        cols = slice(c0, c0 + cw)
        kblk = k_ref[0, pl.ds(pl.multiple_of(j * t, t), t), :]
        qblk = q_sc[mp, :, pl.ds(pl.multiple_of(qi * t + c0, cw), cw)]
        s = _dot(kblk, qblk) + bias_sc[kind, :, cols]
        s_sc[slot, mp, :, cols] = s
        mx_sc[slot, mp, :, cols] = jnp.max(s, axis=0, keepdims=True)

    def softmax_pv(slot, i, mp, c0):
        qi, j = pairs_ref[0, i], pairs_ref[1, i]
        cols = slice(c0, c0 + cw)
        vblk = v_sc[:, pl.ds(pl.multiple_of(j * t, t), t)]
        m_old = m_sc[qi, mp, :, cols]
        m_new = jnp.maximum(m_old, mx_sc[slot, mp, :, cols])
        alpha = jnp.exp2(m_old - m_new)
        p = jnp.exp2((s_sc[slot, mp, :, cols] - m_new).astype(BF16))
        pv = _dot(vblk, p)
        l_sc[qi, mp, :, cols] = alpha * l_sc[qi, mp, :, cols] + pv[V_HEAD_DIM:V_HEAD_DIM + 1]
        acc_sc[qi, mp, :, cols] = alpha * acc_sc[qi, mp, :, cols] + pv[0:V_HEAD_DIM]
        m_sc[qi, mp, :, cols] = m_new

    n_pairs = pairs_ref.shape[1] - 1
    pieces = [(mp, c0) for mp in range(2) for c0 in range(0, t, cw)]
    for mp, c0 in pieces:
        scores(0, 0, mp, c0)

    def stages(first, count):
        for u in range(count):
            for mp, c0 in pieces:
                scores((u + 1) % 2, first + u + 1, mp, c0)
                softmax_pv(u % 2, first + u, mp, c0)

    def trip_body(trip, carry):
        stages(ATTN_STAGES_PER_TRIP * trip, ATTN_STAGES_PER_TRIP)
        return carry

    lax.fori_loop(0, n_pairs // ATTN_STAGES_PER_TRIP, trip_body, 0)
    done = n_pairs - n_pairs % ATTN_STAGES_PER_TRIP
    stages(done, n_pairs - done)

    lv = lam_ref[...]
    lam = (jnp.exp(jnp.sum(lv[0:1] * lv[1:2], axis=-1, keepdims=True))
           - jnp.exp(jnp.sum(lv[2:3] * lv[3:4], axis=-1, keepdims=True)) + lambda_init)
    gain = g_ref[...] * (1.0 - lambda_init)
    for qi in range(nq):
        o = acc_sc[qi, 0] * (1.0 / l_sc[qi, 0]) - acc_sc[qi, 1] * (lam / l_sc[qi, 1])
        o = o * lax.rsqrt(jnp.mean(o * o, axis=0, keepdims=True) + EPS) * gain
        o_ref[0, qi * t:(qi + 1) * t, :] = o.T.astype(BF16)


def _attention(qt, k, vt, rel_bias, lam_vecs, g_subln, lambda_init):
    b, s, _ = k.shape
    t = ATTN_TILE
    assert s % t == 0 and t >= MAX_DISTANCE
    nq = s // t
    hd = V_HEAD_DIM
    pairs = _attn_pairs(nq)
    kern = functools.partial(_attn_kernel, t=t, nq=nq, thresholds=_bucket_thresholds(s),
                             lambda_init=lambda_init)
    return pl.pallas_call(
        kern,
        grid=(N_HEADS, b),
        in_specs=[
            pl.BlockSpec(memory_space=pltpu.SMEM),
            pl.BlockSpec(memory_space=pltpu.SMEM),
            pl.BlockSpec((1, hd, s), lambda h, bi: (bi, h, 0)),
            pl.BlockSpec((1, s, hd), lambda h, bi: (bi, 0, h)),
            pl.BlockSpec((1, hd, s), lambda h, bi: (bi, h, 0)),
            _resident(lam_vecs.shape),
            _resident((hd, 1)),
        ],
        out_specs=pl.BlockSpec((1, s, hd), lambda h, bi: (bi, 0, h)),
        out_shape=jax.ShapeDtypeStruct((b, s, N_HEADS * hd), BF16),
        scratch_shapes=[
            pltpu.VMEM((3, t, t), F32),
            pltpu.VMEM((2, hd, s), BF16),
            pltpu.VMEM((hd + ONES_ROWS, s), BF16),
            pltpu.VMEM((2, 2, t, t), F32),
            pltpu.VMEM((2, 2, 1, t), F32),
            pltpu.VMEM((nq, 2, 1, t), F32),
            pltpu.VMEM((nq, 2, 1, t), F32),
            pltpu.VMEM((nq, 2, hd, t), F32),
        ],
        compiler_params=pltpu.CompilerParams(
            dimension_semantics=("arbitrary", "arbitrary"),
            vmem_limit_bytes=V7X_VMEM_LIMIT),
        name="diff_attention",
    )(rel_bias, jnp.asarray(pairs), qt, k, vt, lam_vecs, g_subln)


def kernel(x, p, g_pre_mix, g_post_mix, g_pre_ffn, g_post_ffn, g_pre_ple, g_post_ple,
           w_sc_in, w_sc_conv, w_sc_out, g_kv, w_kv, rel_bias, w_q, diff_lambda,
           g_subln, w_o, w_ffn_up, w_ffn_conv, w_ffn_down, w_ple_gate, w_ple_proj):
    b, s, d = x.shape
    depth = p.shape[0]
    n_a = depth // 2
    qk_width = N_HEADS * 2 * HEAD_DIM
    x2 = x.reshape(b * s, d)
    p2 = p.reshape(depth, b * s, p.shape[-1])
    bf = lambda w: w.astype(BF16)
    assert n_a >= 1 and w_kv.shape == (d, 2 * qk_width) and qk_width == d

    later = dict(up=w_ffn_up, down=w_ffn_down, gate=w_ple_gate, proj=w_ple_proj, o=w_o, q=w_q, kv=w_kv[None])
    for i in range(depth):
        attn = None
        if i < n_a:
            x2, casted = _mixer(x2, g_pre_mix[i][None], g_post_mix[i][None], bf(w_sc_in[i]),
                                w_sc_conv[i], bf(w_sc_out[i]), s, list(later.values()) if i == 0 else [])
            if i == 0:
                wb = dict(zip(later, casted))
        else:
            j = i - n_a
            assert j == 0, "one attention layer: K/V and Q read the same stream"
            lambda_init = 0.8 - 0.6 * math.exp(-0.3 * i)
            qt, k, vt = _qkv(x2.reshape(b, s, d), g_pre_mix[i][None], g_kv[None], wb["q"][j], wb["kv"][0])
            attn = _attention(qt, k, vt, rel_bias, diff_lambda[j], g_subln[j][:, None],
                              lambda_init).reshape(b * s, d)
        gains = jnp.stack([g_pre_ffn[i], g_post_ffn[i], g_pre_ple[i], g_post_ple[i], g_post_mix[i]])
        x2 = _ffn_ple(x2, p2, i, gains, wb["up"], w_ffn_conv, wb["down"], wb["gate"], wb["proj"], s,
                      attn=attn, w_o=None if attn is None else wb["o"][i - n_a])
    return x2.reshape(b, s, d)
```

```python
import functools
import math

import numpy as np
import jax
import jax.numpy as jnp
from jax import lax
from jax.experimental import pallas as pl
from jax.experimental.pallas import tpu as pltpu

F32 = jnp.float32
BF16 = jnp.bfloat16

EPS = 1e-6
LOG2E = math.log2(math.e)
CONV_WIDTH = 3
N_HEADS = 8
HEAD_DIM = 64
V_HEAD_DIM = 2 * HEAD_DIM
N_BUCKETS = 32
MAX_DISTANCE = 128

V7X_SUBLANES = 8
V7X_MXU_WIDTH = 256
V7X_VMEM_LIMIT = 56 * 1024 * 1024

TOKEN_TILE = 512
SUB_TILE = 256
ATTN_TILE = 512
ATTN_STAGES_PER_TRIP = 12
HALO = V7X_SUBLANES
BF16_ROWS = 2 * V7X_SUBLANES
ONES_ROWS = BF16_ROWS


def _dot(a, b):
    return jnp.dot(a, b, preferred_element_type=F32)


def _dot_nt(a, b):
    return lax.dot_general(a, b, (((1,), (1,)), ((), ())), preferred_element_type=F32)


def _normalize(x):
    return x * lax.rsqrt(jnp.mean(x * x, axis=-1, keepdims=True) + EPS)


def _rms(x, g):
    return _normalize(x) * g


def _sigmoid(x):
    return 1.0 / (1.0 + jnp.exp2(x * -LOG2E))


def _causal_conv_cols(buf, y, wconv_ref, cols, r0):
    rows = y.shape[0]
    buf[HALO + r0:HALO + r0 + rows, cols] = y
    return (wconv_ref[0:1, cols] * buf[HALO + r0 - 2:HALO + r0 - 2 + rows, cols]
            + wconv_ref[1:2, cols] * buf[HALO + r0 - 1:HALO + r0 - 1 + rows, cols]
            + wconv_ref[2:3, cols] * y)


def _emit_staggered(chains):
    depth = max(len(chain) for chain in chains)
    for step in range(depth + len(chains) - 1):
        for lag, chain in enumerate(chains):
            if 0 <= step - lag < len(chain):
                chain[step - lag]()


def _resident(shape, layer=None):
    if layer is None:
        return pl.BlockSpec(shape, lambda *_: (0,) * len(shape), pipeline_mode=pl.Buffered(1))
    return pl.BlockSpec((None,) + tuple(shape), lambda *_: (layer,) + (0,) * len(shape),
                        pipeline_mode=pl.Buffered(1))


def _cast_plan(w, n_steps):
    rows = w.shape[1]
    r = BF16_ROWS * pl.cdiv(pl.cdiv(rows, n_steps), BF16_ROWS)
    while rows % r:
        r += BF16_ROWS
    return r, rows // r


def _mixer_kernel(*refs, tm, tiles_per_seq, d, n_cast):
    x_ref, gpre_ref, gpost_ref, win_ref, wconv_ref, wout_ref = refs[:6]
    cast_in = refs[6:6 + n_cast]
    o_ref = refs[6 + n_cast]
    cast_out = refs[7 + n_cast:7 + 2 * n_cast]
    zbuf, mbuf = refs[7 + 2 * n_cast:]
    i = pl.program_id(0)

    @pl.when(lax.rem(i, tiles_per_seq) == 0)
    def _():
        zbuf[0:HALO, :] = jnp.zeros((HALO, d), F32)

    def cast(src, dst):
        dst[...] = src[...].astype(BF16)

    casts = [functools.partial(cast, src, dst) for src, dst in zip(cast_in, cast_out)]

    cw = V7X_MXU_WIDTH

    def chain(r0):
        rows = slice(r0, r0 + SUB_TILE)
        v = {}

        def pre():
            v["h"] = _rms(x_ref[rows, :], gpre_ref[...]).astype(BF16)

        def chunk(c):
            cols = slice(c * cw, (c + 1) * cw)
            gb = _dot(v["h"], win_ref[:, c * cw:(c + 1) * cw])
            gc = _dot(v["h"], win_ref[:, d + c * cw:d + (c + 1) * cw])
            u = _dot(v["h"], win_ref[:, 2 * d + c * cw:2 * d + (c + 1) * cw])
            conv = _causal_conv_cols(zbuf, gc * u, wconv_ref, cols, r0)
            mbuf[rows, cols] = (gb * conv).astype(BF16)

        def out_proj():
            v["mix"] = _dot(mbuf[rows, :], wout_ref[...])

        def post():
            o_ref[rows, :] = x_ref[rows, :] + _rms(v["mix"], gpost_ref[...])

        return [pre] + [functools.partial(chunk, c) for c in range(d // cw)] + [out_proj, post]

    _emit_staggered([chain(r0) for r0 in range(0, tm, SUB_TILE)] + [casts])
    zbuf[0:HALO, :] = zbuf[tm:tm + HALO, :]


def _mixer(x2, g_pre, g_post, w_in, w_conv, w_out, seq, to_cast):
    t, d = x2.shape
    tm = TOKEN_TILE
    n_steps = t // tm
    plans = [_cast_plan(w, n_steps) for w in to_cast]
    kern = functools.partial(_mixer_kernel, tm=tm, tiles_per_seq=seq // tm, d=d, n_cast=len(plans))
    row = pl.BlockSpec((tm, d), lambda i: (i, 0))
    cast_specs = [pl.BlockSpec((w.shape[0], r, w.shape[2]), lambda i, nb=nb: (0, jnp.minimum(i, nb - 1), 0))
                  for w, (r, nb) in zip(to_cast, plans)]
    outs = pl.pallas_call(
        kern,
        grid=(n_steps,),
        in_specs=[row, _resident((1, d)), _resident((1, d)), _resident((d, 3 * d)),
                  _resident((CONV_WIDTH, d)), _resident((d, d))] + cast_specs,
        out_specs=[row] + cast_specs,
        out_shape=[jax.ShapeDtypeStruct((t, d), F32)] + [jax.ShapeDtypeStruct(w.shape, BF16) for w in to_cast],
        scratch_shapes=[pltpu.VMEM((tm + HALO, d), F32), pltpu.VMEM((tm, d), BF16)],
        compiler_params=pltpu.CompilerParams(
            dimension_semantics=("arbitrary",), vmem_limit_bytes=V7X_VMEM_LIMIT),
        name="sconv_mixer",
    )(x2, g_pre, g_post, w_in, w_conv, w_out, *to_cast)
    return outs[0], outs[1:]


def _ffn_ple_kernel(*refs, tm, tiles_per_seq, d, f, with_attn_out):
    if with_attn_out:
        (x_ref, p_ref, a_ref, gains_ref, wo_ref, wup_ref, wconv_ref, wdown_ref, wgate_ref,
         wproj_ref, o_ref, ybuf, abuf) = refs
    else:
        (x_ref, p_ref, gains_ref, wup_ref, wconv_ref, wdown_ref, wgate_ref,
         wproj_ref, o_ref, ybuf, abuf) = refs
    i = pl.program_id(0)

    @pl.when(lax.rem(i, tiles_per_seq) == 0)
    def _():
        ybuf[0:HALO, :] = jnp.zeros((HALO, 2 * f), F32)

    cw = V7X_MXU_WIDTH

    def chain(r0):
        rows = slice(r0, r0 + SUB_TILE)
        v = {}

        def pre():
            x = x_ref[rows, :]
            if with_attn_out:
                x = x + _rms(_dot(a_ref[rows, :], wo_ref[...]), gains_ref[4:5, :])
            v["x"] = x
            v["hn"] = _rms(x, gains_ref[0:1, :]).astype(BF16)

        def chunk(c):
            gcols = slice(c * cw, (c + 1) * cw)
            ucols = slice(f + c * cw, f + (c + 1) * cw)
            gate = _causal_conv_cols(ybuf, _dot(v["hn"], wup_ref[:, c * cw:(c + 1) * cw]),
                                     wconv_ref, gcols, r0)
            up = _causal_conv_cols(ybuf, _dot(v["hn"], wup_ref[:, f + c * cw:f + (c + 1) * cw]),
                                   wconv_ref, ucols, r0)
            abuf[rows, gcols] = (gate * _sigmoid(gate) * up).astype(BF16)

        def down():
            v["ffn"] = _dot(abuf[rows, :], wdown_ref[...])

        def post_ffn():
            v["x"] = v["x"] + _rms(v["ffn"], gains_ref[1:2, :])
            v["hg"] = _rms(v["x"], gains_ref[2:3, :]).astype(BF16)

        def embed():
            v["gate"] = _dot(v["hg"], wgate_ref[...])
            v["emb"] = _dot(p_ref[rows, :].astype(BF16), wproj_ref[...])

        def post_embed():
            o_ref[rows, :] = v["x"] + _rms(_sigmoid(v["gate"]) * v["emb"], gains_ref[3:4, :])

        return ([pre] + [functools.partial(chunk, c) for c in range(f // cw)]
                + [down, post_ffn, embed, post_embed])

    _emit_staggered([chain(r0) for r0 in range(0, tm, SUB_TILE)])
    ybuf[0:HALO, :] = ybuf[tm:tm + HALO, :]


def _ffn_ple(x2, p3, layer, gains, w_up, w_conv, w_down, w_gate, w_proj, seq, attn=None, w_o=None):
    t, d = x2.shape
    f = w_down.shape[1]
    pd = p3.shape[2]
    tm = TOKEN_TILE
    with_attn_out = attn is not None
    kern = functools.partial(_ffn_ple_kernel, tm=tm, tiles_per_seq=seq // tm, d=d, f=f,
                             with_attn_out=with_attn_out)
    row = pl.BlockSpec((tm, d), lambda i: (i, 0))
    prow = pl.BlockSpec((None, tm, pd), lambda i: (layer, i, 0))
    args = [x2, p3]
    specs = [row, prow]
    if with_attn_out:
        args.append(attn)
        specs.append(row)
    args.append(gains)
    specs.append(_resident(gains.shape))
    if with_attn_out:
        args.append(w_o)
        specs.append(_resident((d, d)))
    args += [w_up, w_conv, w_down, w_gate, w_proj]
    specs += [_resident((d, 2 * f), layer), _resident((CONV_WIDTH, 2 * f), layer), _resident((f, d), layer),
              _resident((d, d), layer), _resident((pd, d), layer)]
    return pl.pallas_call(
        kern,
        grid=(t // tm,),
        in_specs=specs,
        out_specs=row,
        out_shape=jax.ShapeDtypeStruct((t, d), F32),
        scratch_shapes=[pltpu.VMEM((tm + HALO, 2 * f), F32), pltpu.VMEM((tm, f), BF16)],
        compiler_params=pltpu.CompilerParams(
            dimension_semantics=("arbitrary",), vmem_limit_bytes=V7X_VMEM_LIMIT),
        name="attn_out_ffn_ple" if with_attn_out else "ffn_ple",
    )(*args)


def _qkv_kernel(x_ref, gq_ref, gkv_ref, wq_ref, wk_ref, wv_ref, qt_ref, k_ref, vt_ref, wqt_ref, wvt_ref):
    @pl.when((pl.program_id(0) == 0) & (pl.program_id(1) == 0))
    def _():
        wqt_ref[...] = wq_ref[...].T
        wvt_ref[...] = wv_ref[...].T

    def chain(r0):
        rows = slice(r0, r0 + SUB_TILE)
        v = {}

        def pre():
            xn = _normalize(x_ref[0, rows, :])
            v["h"] = (xn * gq_ref[...]).astype(BF16)
            v["hk"] = (xn * gkv_ref[...]).astype(BF16)

        def q_proj():
            qt_ref[0, :, rows] = (_dot_nt(wqt_ref[...], v["h"]) * (HEAD_DIM ** -0.5 * LOG2E)).astype(BF16)

        def k_proj():
            k_ref[0, rows, :] = _dot(v["hk"], wk_ref[...]).astype(BF16)

        def v_proj():
            vt_ref[0, :, rows] = _dot_nt(wvt_ref[...], v["hk"]).astype(BF16)

        return [pre, q_proj, k_proj, v_proj]

    _emit_staggered([chain(r0) for r0 in range(0, x_ref.shape[1], SUB_TILE)])


def _qkv(x3, g_q, g_kv, w_q, w_kv):
    b, s, d = x3.shape
    tm = TOKEN_TILE
    row = pl.BlockSpec((1, tm, d), lambda bi, i: (bi, i, 0))
    col = pl.BlockSpec((1, d, tm), lambda bi, i: (bi, 0, i))
    w_k_spec = pl.BlockSpec((d, d), lambda *_: (0, 0), pipeline_mode=pl.Buffered(1))
    w_v_spec = pl.BlockSpec((d, d), lambda *_: (0, 1), pipeline_mode=pl.Buffered(1))
    return pl.pallas_call(
        _qkv_kernel,
        grid=(b, s // tm),
        in_specs=[row, _resident((1, d)), _resident((1, d)), _resident((d, d)), w_k_spec, w_v_spec],
        out_specs=[col, row, col],
        out_shape=[jax.ShapeDtypeStruct((b, d, s), BF16), jax.ShapeDtypeStruct((b, s, d), BF16),
                   jax.ShapeDtypeStruct((b, d, s), BF16)],
        scratch_shapes=[pltpu.VMEM((d, d), BF16), pltpu.VMEM((d, d), BF16)],
        compiler_params=pltpu.CompilerParams(
            dimension_semantics=("arbitrary", "arbitrary"), vmem_limit_bytes=V7X_VMEM_LIMIT),
        name="qkv_proj",
    )(x3, g_q, g_kv, w_q, w_kv, w_kv)


def _bucket_thresholds(seq):
    dist = np.arange(seq)
    max_exact = N_BUCKETS // 2
    large = max_exact + (np.log(np.maximum(dist, 1).astype(np.float32) / np.float32(max_exact))
                         / np.float32(math.log(MAX_DISTANCE / max_exact))
                         * np.float32(N_BUCKETS - max_exact)).astype(np.int32)
    bucket = np.where(dist < max_exact, dist, np.minimum(large, N_BUCKETS - 1))
    assert np.all(np.diff(bucket) >= 0)
    thr = [int(np.argmax(bucket >= b)) if np.any(bucket >= b) else seq for b in range(N_BUCKETS)]
    assert np.all(bucket[MAX_DISTANCE:] == N_BUCKETS - 1)
    return thr


def _attn_pairs(nq):
    pairs = [(qi, qi, 0) for qi in range(nq)]
    pairs += [(qi, qi - n, min(n, 2)) for qi in range(nq) for n in range(1, qi + 1)]
    return np.asarray(pairs, np.int32).T


def _attn_kernel(rb_ref, pairs_ref, qt_ref, k_ref, vt_ref, lam_ref, g_ref, o_ref,
                 bias_sc, q_sc, v_sc, s_sc, mx_sc, m_sc, l_sc, acc_sc,
                 *, t, nq, thresholds, lambda_init):
    h = pl.program_id(0)
    bi = pl.program_id(1)

    @pl.when(bi == 0)
    def _():
        row = lax.broadcasted_iota(jnp.int32, (t, t), 0)
        col = lax.broadcasted_iota(jnp.int32, (t, t), 1)
        for tile, delta in ((0, 0), (1, t)):
            dist = col - row + delta
            val = jnp.full((t, t), rb_ref[0, h], F32)
            for b in range(1, N_BUCKETS):
                val = jnp.where(dist >= thresholds[b], rb_ref[b, h], val)
            val = val * LOG2E
            if tile == 0:
                val = jnp.where(dist >= 0, val, -jnp.inf)
            bias_sc[tile] = val
        bias_sc[2] = jnp.full((t, t), rb_ref[N_BUCKETS - 1, h] * LOG2E, F32)
        zero = jnp.zeros((HEAD_DIM, nq * t), BF16)
        q_sc[0, HEAD_DIM:, :] = zero
        q_sc[1, 0:HEAD_DIM, :] = zero
        v_sc[V_HEAD_DIM:, :] = jnp.ones((ONES_ROWS, nq * t), BF16)
        l_sc[...] = jnp.zeros(l_sc.shape, F32)
        acc_sc[...] = jnp.zeros(acc_sc.shape, F32)

    q_sc[0, 0:HEAD_DIM, :] = qt_ref[0, 0:HEAD_DIM, :]
    q_sc[1, HEAD_DIM:, :] = qt_ref[0, HEAD_DIM:, :]
    v_sc[0:V_HEAD_DIM, :] = vt_ref[0]
    m_sc[...] = jnp.full(m_sc.shape, -jnp.inf, F32)

    cw = V7X_MXU_WIDTH

    def pair(i):
        if isinstance(i, int) and i < nq:
            return i, i, 0, True
        if isinstance(i, int) and i >= n_pairs:
            return None
        return pairs_ref[0, i], pairs_ref[1, i], pairs_ref[2, i], False

    def aligned(start):
        return start if isinstance(start, int) else pl.multiple_of(start, cw)

    def keys_used(diagonal, c0):
        return c0 + cw if diagonal else t

    def scores(slot, pr, mp, c0):
        qi, j, kind, diagonal = pr
        cols, nk = slice(c0, c0 + cw), keys_used(diagonal, c0)
        kblk = k_ref[0, pl.ds(aligned(j * t), nk), :]
        qblk = q_sc[mp, :, pl.ds(aligned(qi * t + c0), cw)]
        s = _dot(kblk, qblk) + bias_sc[kind, 0:nk, cols]
        s_sc[slot, mp, 0:nk, cols] = s
        mx_sc[slot, mp, :, cols] = jnp.max(s, axis=0, keepdims=True)

    def softmax_pv(slot, pr, mp, c0):
        qi, j, _, diagonal = pr
        cols, nk = slice(c0, c0 + cw), keys_used(diagonal, c0)
        vblk = v_sc[:, pl.ds(aligned(j * t), nk)]
        m_old = m_sc[qi, mp, :, cols]
        m_new = jnp.maximum(m_old, mx_sc[slot, mp, :, cols])
        alpha = jnp.exp2(m_old - m_new)
        p = jnp.exp2((s_sc[slot, mp, 0:nk, cols] - m_new).astype(BF16))
        pv = _dot(vblk, p)
        l_sc[qi, mp, :, cols] = alpha * l_sc[qi, mp, :, cols] + pv[V_HEAD_DIM:V_HEAD_DIM + 1]
        acc_sc[qi, mp, :, cols] = alpha * acc_sc[qi, mp, :, cols] + pv[0:V_HEAD_DIM]
        m_sc[qi, mp, :, cols] = m_new

    n_pairs = pairs_ref.shape[1]
    pieces = [(mp, c0) for mp in range(2) for c0 in range(0, t, cw)]

    def stage(i, parity):
        for mp, c0 in pieces:
            if pair(i + 1) is not None:
                scores(1 - parity, pair(i + 1), mp, c0)
            softmax_pv(parity, pair(i), mp, c0)

    for mp, c0 in pieces:
        scores(0, pair(0), mp, c0)
    for i in range(nq):
        stage(i, i % 2)

    def trip_body(trip, carry):
        first = nq + ATTN_STAGES_PER_TRIP * trip
        for u in range(ATTN_STAGES_PER_TRIP):
            stage(first + u, (nq + u) % 2)
        return carry

    trips = (n_pairs - 1 - nq) // ATTN_STAGES_PER_TRIP
    lax.fori_loop(0, trips, trip_body, 0)
    for i in range(nq + trips * ATTN_STAGES_PER_TRIP, n_pairs):
        stage(i, i % 2)

    lv = lam_ref[...]
    lam = (jnp.exp(jnp.sum(lv[0:1] * lv[1:2], axis=-1, keepdims=True))
           - jnp.exp(jnp.sum(lv[2:3] * lv[3:4], axis=-1, keepdims=True)) + lambda_init)
    gain = g_ref[...] * (1.0 - lambda_init)
    for qi in range(nq):
        o = acc_sc[qi, 0] * (1.0 / l_sc[qi, 0]) - acc_sc[qi, 1] * (lam / l_sc[qi, 1])
        o = o * lax.rsqrt(jnp.mean(o * o, axis=0, keepdims=True) + EPS) * gain
        o_ref[0, qi * t:(qi + 1) * t, :] = o.T.astype(BF16)


def _attention(qt, k, vt, rel_bias, lam_vecs, g_subln, lambda_init):
    b, s, _ = k.shape
    t = ATTN_TILE
    assert s % t == 0 and t >= MAX_DISTANCE
    nq = s // t
    hd = V_HEAD_DIM
    pairs = _attn_pairs(nq)
    kern = functools.partial(_attn_kernel, t=t, nq=nq, thresholds=_bucket_thresholds(s),
                             lambda_init=lambda_init)
    return pl.pallas_call(
        kern,
        grid=(N_HEADS, b),
        in_specs=[
            pl.BlockSpec(memory_space=pltpu.SMEM),
            pl.BlockSpec(memory_space=pltpu.SMEM),
            pl.BlockSpec((1, hd, s), lambda h, bi: (bi, h, 0)),
            pl.BlockSpec((1, s, hd), lambda h, bi: (bi, 0, h)),
            pl.BlockSpec((1, hd, s), lambda h, bi: (bi, h, 0)),
            _resident(lam_vecs.shape),
            _resident((hd, 1)),
        ],
        out_specs=pl.BlockSpec((1, s, hd), lambda h, bi: (bi, 0, h)),
        out_shape=jax.ShapeDtypeStruct((b, s, N_HEADS * hd), BF16),
        scratch_shapes=[
            pltpu.VMEM((3, t, t), F32),
            pltpu.VMEM((2, hd, s), BF16),
            pltpu.VMEM((hd + ONES_ROWS, s), BF16),
            pltpu.VMEM((2, 2, t, t), F32),
            pltpu.VMEM((2, 2, 1, t), F32),
            pltpu.VMEM((nq, 2, 1, t), F32),
            pltpu.VMEM((nq, 2, 1, t), F32),
            pltpu.VMEM((nq, 2, hd, t), F32),
        ],
        compiler_params=pltpu.CompilerParams(
            dimension_semantics=("arbitrary", "arbitrary"),
            vmem_limit_bytes=V7X_VMEM_LIMIT),
        name="diff_attention",
    )(rel_bias, jnp.asarray(pairs), qt, k, vt, lam_vecs, g_subln)


def kernel(x, p, g_pre_mix, g_post_mix, g_pre_ffn, g_post_ffn, g_pre_ple, g_post_ple,
           w_sc_in, w_sc_conv, w_sc_out, g_kv, w_kv, rel_bias, w_q, diff_lambda,
           g_subln, w_o, w_ffn_up, w_ffn_conv, w_ffn_down, w_ple_gate, w_ple_proj):
    b, s, d = x.shape
    depth = p.shape[0]
    n_a = depth // 2
    qk_width = N_HEADS * 2 * HEAD_DIM
    x2 = x.reshape(b * s, d)
    p2 = p.reshape(depth, b * s, p.shape[-1])
    bf = lambda w: w.astype(BF16)
    assert n_a >= 1 and w_kv.shape == (d, 2 * qk_width) and qk_width == d

    later = dict(up=w_ffn_up, down=w_ffn_down, gate=w_ple_gate, proj=w_ple_proj, o=w_o, q=w_q, kv=w_kv[None])
    for i in range(depth):
        attn = None
        if i < n_a:
            x2, casted = _mixer(x2, g_pre_mix[i][None], g_post_mix[i][None], bf(w_sc_in[i]),
                                w_sc_conv[i], bf(w_sc_out[i]), s, list(later.values()) if i == 0 else [])
            if i == 0:
                wb = dict(zip(later, casted))
        else:
            j = i - n_a
            assert j == 0, "one attention layer: K/V and Q read the same stream"
            lambda_init = 0.8 - 0.6 * math.exp(-0.3 * i)
            qt, k, vt = _qkv(x2.reshape(b, s, d), g_pre_mix[i][None], g_kv[None], wb["q"][j], wb["kv"][0])
            attn = _attention(qt, k, vt, rel_bias, diff_lambda[j], g_subln[j][:, None],
                              lambda_init).reshape(b * s, d)
        gains = jnp.stack([g_pre_ffn[i], g_post_ffn[i], g_pre_ple[i], g_post_ple[i], g_post_mix[i]])
        x2 = _ffn_ple(x2, p2, i, gains, wb["up"], w_ffn_conv, wb["down"], wb["gate"], wb["proj"], s,
                      attn=attn, w_o=None if attn is None else wb["o"][i - n_a])
    return x2.reshape(b, s, d)
```

```python
import functools
import math

import numpy as np
import jax
import jax.numpy as jnp
from jax import lax
from jax.experimental import pallas as pl
from jax.experimental.pallas import tpu as pltpu

F32 = jnp.float32
BF16 = jnp.bfloat16

EPS = 1e-6
LOG2E = math.log2(math.e)
CONV_WIDTH = 3
N_HEADS = 8
HEAD_DIM = 64
V_HEAD_DIM = 2 * HEAD_DIM
N_BUCKETS = 32
MAX_DISTANCE = 128

V7X_SUBLANES = 8
V7X_MXU_WIDTH = 256
V7X_VMEM_LIMIT = 56 * 1024 * 1024

TOKEN_TILE = 512
SUB_TILE = 256
ATTN_TILE = 512
ATTN_STAGES_PER_TRIP = 12
HALO = V7X_SUBLANES
BF16_ROWS = 2 * V7X_SUBLANES
ONES_ROWS = BF16_ROWS


def _dot(a, b):
    return jnp.dot(a, b, preferred_element_type=F32)


def _dot_nt(a, b):
    return lax.dot_general(a, b, (((1,), (1,)), ((), ())), preferred_element_type=F32)


def _normalize(x):
    return x * lax.rsqrt(jnp.mean(x * x, axis=-1, keepdims=True) + EPS)


def _rms(x, g):
    return _normalize(x) * g


def _sigmoid(x):
    return 1.0 / (1.0 + jnp.exp2(x * -LOG2E))


def _causal_conv_cols(buf, y, wconv_ref, cols, r0):
    rows = y.shape[0]
    buf[HALO + r0:HALO + r0 + rows, cols] = y
    return (wconv_ref[0:1, cols] * buf[HALO + r0 - 2:HALO + r0 - 2 + rows, cols]
            + wconv_ref[1:2, cols] * buf[HALO + r0 - 1:HALO + r0 - 1 + rows, cols]
            + wconv_ref[2:3, cols] * y)


def _emit_staggered(chains):
    depth = max(len(chain) for chain in chains)
    for step in range(depth + len(chains) - 1):
        for lag, chain in enumerate(chains):
            if 0 <= step - lag < len(chain):
                chain[step - lag]()


def _alternate(a, b):
    out = []
    for i in range(max(len(a), len(b))):
        out += a[i:i + 1] + b[i:i + 1]
    return out


def _resident(shape, layer=None):
    if layer is None:
        return pl.BlockSpec(shape, lambda *_: (0,) * len(shape), pipeline_mode=pl.Buffered(1))
    return pl.BlockSpec((None,) + tuple(shape), lambda *_: (layer,) + (0,) * len(shape),
                        pipeline_mode=pl.Buffered(1))


def _cast_plan(w, n_steps):
    rows = w.shape[1]
    r = BF16_ROWS * pl.cdiv(pl.cdiv(rows, n_steps), BF16_ROWS)
    while rows % r:
        r += BF16_ROWS
    return r, rows // r


def _mixer_kernel(*refs, tm, tiles_per_seq, d, n_cast):
    x_ref, gpre_ref, gpost_ref, win_ref, wconv_ref, wout_ref = refs[:6]
    cast_in = refs[6:6 + n_cast]
    o_ref = refs[6 + n_cast]
    cast_out = refs[7 + n_cast:7 + 2 * n_cast]
    zbuf, mbuf = refs[7 + 2 * n_cast:]
    i = pl.program_id(0)

    @pl.when(lax.rem(i, tiles_per_seq) == 0)
    def _():
        zbuf[0:HALO, :] = jnp.zeros((HALO, d), F32)

    def cast(src, dst):
        dst[...] = src[...].astype(BF16)

    casts = [functools.partial(cast, src, dst) for src, dst in zip(cast_in, cast_out)]

    cw = V7X_MXU_WIDTH

    def chain(r0):
        rows = slice(r0, r0 + SUB_TILE)
        v = {}

        def pre():
            v["h"] = _rms(x_ref[rows, :], gpre_ref[...]).astype(BF16)

        def chunk(c):
            cols = slice(c * cw, (c + 1) * cw)
            gb = _dot(v["h"], win_ref[:, c * cw:(c + 1) * cw])
            gc = _dot(v["h"], win_ref[:, d + c * cw:d + (c + 1) * cw])
            u = _dot(v["h"], win_ref[:, 2 * d + c * cw:2 * d + (c + 1) * cw])
            conv = _causal_conv_cols(zbuf, gc * u, wconv_ref, cols, r0)
            mbuf[rows, cols] = (gb * conv).astype(BF16)

        def out_proj(k):
            v.setdefault("mix", []).append(_dot(mbuf[rows, :], wout_ref[:, k * cw:(k + 1) * cw]))

        def post():
            mix = jnp.concatenate(v.pop("mix"), axis=1)
            o_ref[rows, :] = x_ref[rows, :] + _rms(mix, gpost_ref[...])

        front = [pre] + [functools.partial(chunk, c) for c in range(d // cw)]
        return front, [functools.partial(out_proj, k) for k in range(d // cw)], post

    chains = [chain(r0) for r0 in range(0, tm, SUB_TILE)]
    _emit_staggered([front for front, _, _ in chains] + [casts])
    (_, mm_a, post_a), (_, mm_b, post_b) = chains
    for piece in mm_a + _alternate(mm_b, [post_a]) + [post_b]:
        piece()
    zbuf[0:HALO, :] = zbuf[tm:tm + HALO, :]


def _mixer(x2, g_pre, g_post, w_in, w_conv, w_out, seq, to_cast):
    t, d = x2.shape
    tm = TOKEN_TILE
    n_steps = t // tm
    plans = [_cast_plan(w, n_steps) for w in to_cast]
    kern = functools.partial(_mixer_kernel, tm=tm, tiles_per_seq=seq // tm, d=d, n_cast=len(plans))
    row = pl.BlockSpec((tm, d), lambda i: (i, 0))
    cast_specs = [pl.BlockSpec((w.shape[0], r, w.shape[2]), lambda i, nb=nb: (0, jnp.minimum(i, nb - 1), 0))
                  for w, (r, nb) in zip(to_cast, plans)]
    outs = pl.pallas_call(
        kern,
        grid=(n_steps,),
        in_specs=[row, _resident((1, d)), _resident((1, d)), _resident((d, 3 * d)),
                  _resident((CONV_WIDTH, d)), _resident((d, d))] + cast_specs,
        out_specs=[row] + cast_specs,
        out_shape=[jax.ShapeDtypeStruct((t, d), F32)] + [jax.ShapeDtypeStruct(w.shape, BF16) for w in to_cast],
        scratch_shapes=[pltpu.VMEM((tm + HALO, d), F32), pltpu.VMEM((tm, d), BF16)],
        compiler_params=pltpu.CompilerParams(
            dimension_semantics=("arbitrary",), vmem_limit_bytes=V7X_VMEM_LIMIT),
        name="sconv_mixer",
    )(x2, g_pre, g_post, w_in, w_conv, w_out, *to_cast)
    return outs[0], outs[1:]


def _ffn_ple_kernel(*refs, tm, tiles_per_seq, d, f, with_attn_out):
    if with_attn_out:
        (x_ref, p_ref, a_ref, gains_ref, wo_ref, wup_ref, wconv_ref, wdown_ref, wgate_ref,
         wproj_ref, o_ref, ybuf, abuf) = refs
    else:
        (x_ref, p_ref, gains_ref, wup_ref, wconv_ref, wdown_ref, wgate_ref,
         wproj_ref, o_ref, ybuf, abuf) = refs
    i = pl.program_id(0)

    @pl.when(lax.rem(i, tiles_per_seq) == 0)
    def _():
        ybuf[0:HALO, :] = jnp.zeros((HALO, 2 * f), F32)

    cw = V7X_MXU_WIDTH

    def chain(r0):
        rows = slice(r0, r0 + SUB_TILE)
        v = {}

        def pre():
            x = x_ref[rows, :]
            if with_attn_out:
                x = x + _rms(_dot(a_ref[rows, :], wo_ref[...]), gains_ref[4:5, :])
            v["x"] = x
            v["hn"] = _rms(x, gains_ref[0:1, :]).astype(BF16)

        def chunk(c):
            gcols = slice(c * cw, (c + 1) * cw)
            ucols = slice(f + c * cw, f + (c + 1) * cw)
            gate = _causal_conv_cols(ybuf, _dot(v["hn"], wup_ref[:, c * cw:(c + 1) * cw]),
                                     wconv_ref, gcols, r0)
            up = _causal_conv_cols(ybuf, _dot(v["hn"], wup_ref[:, f + c * cw:f + (c + 1) * cw]),
                                   wconv_ref, ucols, r0)
            abuf[rows, gcols] = (gate * _sigmoid(gate) * up).astype(BF16)

        def down(k):
            v.setdefault("ffn", []).append(_dot(abuf[rows, :], wdown_ref[:, k * cw:(k + 1) * cw]))

        def ffn_norm():
            v["x"] = v["x"] + _rms(jnp.concatenate(v.pop("ffn"), axis=1), gains_ref[1:2, :])

        def embed_norm():
            v["hg"] = _rms(v["x"], gains_ref[2:3, :]).astype(BF16)

        def gate(k):
            v.setdefault("gate", []).append(_dot(v["hg"], wgate_ref[:, k * cw:(k + 1) * cw]))

        def proj():
            v["emb"] = _dot(p_ref[rows, :].astype(BF16), wproj_ref[...])

        def gated():
            v["ple"] = _sigmoid(jnp.concatenate(v.pop("gate"), axis=1)) * v.pop("emb")

        def store():
            o_ref[rows, :] = v["x"] + _rms(v.pop("ple"), gains_ref[3:4, :])

        nk = d // cw
        front = [pre] + [functools.partial(chunk, c) for c in range(f // cw)]
        matmuls = ([functools.partial(down, k) for k in range(nk)]
                   + [functools.partial(gate, k) for k in range(nk)] + [proj])
        elementwise = [ffn_norm, embed_norm, gated, store]
        return front, matmuls, elementwise, nk

    chains = [chain(r0) for r0 in range(0, tm, SUB_TILE)]
    _emit_staggered([front for front, _, _, _ in chains])
    (_, mm_a, ew_a, nk), (_, mm_b, ew_b, _) = chains
    order = mm_a[:nk]
    order += _alternate(mm_b[:nk], ew_a[:2])
    order += _alternate(mm_a[nk:], ew_b[:2])
    order += _alternate(mm_b[nk:], ew_a[2:])
    order += ew_b[2:]
    for piece in order:
        piece()
    ybuf[0:HALO, :] = ybuf[tm:tm + HALO, :]


def _ffn_ple(x2, p3, layer, gains, w_up, w_conv, w_down, w_gate, w_proj, seq, attn=None, w_o=None):
    t, d = x2.shape
    f = w_down.shape[1]
    pd = p3.shape[2]
    tm = TOKEN_TILE
    with_attn_out = attn is not None
    kern = functools.partial(_ffn_ple_kernel, tm=tm, tiles_per_seq=seq // tm, d=d, f=f,
                             with_attn_out=with_attn_out)
    row = pl.BlockSpec((tm, d), lambda i: (i, 0))
    prow = pl.BlockSpec((None, tm, pd), lambda i: (layer, i, 0))
    args = [x2, p3]
    specs = [row, prow]
    if with_attn_out:
        args.append(attn)
        specs.append(row)
    args.append(gains)
    specs.append(_resident(gains.shape))
    if with_attn_out:
        args.append(w_o)
        specs.append(_resident((d, d)))
    args += [w_up, w_conv, w_down, w_gate, w_proj]
    specs += [_resident((d, 2 * f), layer), _resident((CONV_WIDTH, 2 * f), layer), _resident((f, d), layer),
              _resident((d, d), layer), _resident((pd, d), layer)]
    return pl.pallas_call(
        kern,
        grid=(t // tm,),
        in_specs=specs,
        out_specs=row,
        out_shape=jax.ShapeDtypeStruct((t, d), F32),
        scratch_shapes=[pltpu.VMEM((tm + HALO, 2 * f), F32), pltpu.VMEM((tm, f), BF16)],
        compiler_params=pltpu.CompilerParams(
            dimension_semantics=("arbitrary",), vmem_limit_bytes=V7X_VMEM_LIMIT),
        name="attn_out_ffn_ple" if with_attn_out else "ffn_ple",
    )(*args)


def _qkv_kernel(x_ref, gq_ref, gkv_ref, wq_ref, wk_ref, wv_ref, qt_ref, k_ref, vt_ref, wqt_ref, wvt_ref):
    @pl.when((pl.program_id(0) == 0) & (pl.program_id(1) == 0))
    def _():
        wqt_ref[...] = wq_ref[...].T
        wvt_ref[...] = wv_ref[...].T

    def chain(r0):
        rows = slice(r0, r0 + SUB_TILE)
        v = {}

        def pre():
            xn = _normalize(x_ref[0, rows, :])
            v["h"] = (xn * gq_ref[...]).astype(BF16)
            v["hk"] = (xn * gkv_ref[...]).astype(BF16)

        def q_proj():
            qt_ref[0, :, rows] = (_dot_nt(wqt_ref[...], v["h"]) * (HEAD_DIM ** -0.5 * LOG2E)).astype(BF16)

        def k_proj():
            k_ref[0, rows, :] = _dot(v["hk"], wk_ref[...]).astype(BF16)

        def v_proj():
            vt_ref[0, :, rows] = _dot_nt(wvt_ref[...], v["hk"]).astype(BF16)

        return [pre, q_proj, k_proj, v_proj]

    _emit_staggered([chain(r0) for r0 in range(0, x_ref.shape[1], SUB_TILE)])


def _qkv(x3, g_q, g_kv, w_q, w_kv):
    b, s, d = x3.shape
    tm = TOKEN_TILE
    row = pl.BlockSpec((1, tm, d), lambda bi, i: (bi, i, 0))
    col = pl.BlockSpec((1, d, tm), lambda bi, i: (bi, 0, i))
    w_k_spec = pl.BlockSpec((d, d), lambda *_: (0, 0), pipeline_mode=pl.Buffered(1))
    w_v_spec = pl.BlockSpec((d, d), lambda *_: (0, 1), pipeline_mode=pl.Buffered(1))
    return pl.pallas_call(
        _qkv_kernel,
        grid=(b, s // tm),
        in_specs=[row, _resident((1, d)), _resident((1, d)), _resident((d, d)), w_k_spec, w_v_spec],
        out_specs=[col, row, col],
        out_shape=[jax.ShapeDtypeStruct((b, d, s), BF16), jax.ShapeDtypeStruct((b, s, d), BF16),
                   jax.ShapeDtypeStruct((b, d, s), BF16)],
        scratch_shapes=[pltpu.VMEM((d, d), BF16), pltpu.VMEM((d, d), BF16)],
        compiler_params=pltpu.CompilerParams(
            dimension_semantics=("arbitrary", "arbitrary"), vmem_limit_bytes=V7X_VMEM_LIMIT),
        name="qkv_proj",
    )(x3, g_q, g_kv, w_q, w_kv, w_kv)


def _bucket_thresholds(seq):
    dist = np.arange(seq)
    max_exact = N_BUCKETS // 2
    large = max_exact + (np.log(np.maximum(dist, 1).astype(np.float32) / np.float32(max_exact))
                         / np.float32(math.log(MAX_DISTANCE / max_exact))
                         * np.float32(N_BUCKETS - max_exact)).astype(np.int32)
    bucket = np.where(dist < max_exact, dist, np.minimum(large, N_BUCKETS - 1))
    assert np.all(np.diff(bucket) >= 0)
    thr = [int(np.argmax(bucket >= b)) if np.any(bucket >= b) else seq for b in range(N_BUCKETS)]
    assert np.all(bucket[MAX_DISTANCE:] == N_BUCKETS - 1)
    return thr


def _attn_pairs(nq):
    pairs = [(qi, qi, 0) for qi in range(nq)]
    pairs += [(qi, qi - n, min(n, 2)) for qi in range(nq) for n in range(1, qi + 1)]
    return np.asarray(pairs, np.int32).T


def _attn_kernel(rb_ref, pairs_ref, qt_ref, k_ref, vt_ref, lam_ref, g_ref, o_ref,
                 bias_sc, q_sc, v_sc, s_sc, mx_sc, m_sc, l_sc, acc_sc,
                 *, t, nq, thresholds, lambda_init):
    h = pl.program_id(0)
    bi = pl.program_id(1)

    @pl.when(bi == 0)
    def _():
        row = lax.broadcasted_iota(jnp.int32, (t, t), 0)
        col = lax.broadcasted_iota(jnp.int32, (t, t), 1)
        for tile, delta in ((0, 0), (1, t)):
            dist = col - row + delta
            val = jnp.full((t, t), rb_ref[0, h], F32)
            for b in range(1, N_BUCKETS):
                val = jnp.where(dist >= thresholds[b], rb_ref[b, h], val)
            val = val * LOG2E
            if tile == 0:
                val = jnp.where(dist >= 0, val, -jnp.inf)
            bias_sc[tile] = val
        bias_sc[2] = jnp.full((t, t), rb_ref[N_BUCKETS - 1, h] * LOG2E, F32)
        zero = jnp.zeros((HEAD_DIM, nq * t), BF16)
        q_sc[0, HEAD_DIM:, :] = zero
        q_sc[1, 0:HEAD_DIM, :] = zero
        v_sc[V_HEAD_DIM:, :] = jnp.ones((ONES_ROWS, nq * t), BF16)
        l_sc[...] = jnp.zeros(l_sc.shape, F32)
        acc_sc[...] = jnp.zeros(acc_sc.shape, F32)

    q_sc[0, 0:HEAD_DIM, :] = qt_ref[0, 0:HEAD_DIM, :]
    q_sc[1, HEAD_DIM:, :] = qt_ref[0, HEAD_DIM:, :]
    v_sc[0:V_HEAD_DIM, :] = vt_ref[0]
    m_sc[...] = jnp.full(m_sc.shape, -jnp.inf, F32)

    cw = V7X_MXU_WIDTH

    def pair(i):
        if isinstance(i, int) and i < nq:
            return i, i, 0, True
        if isinstance(i, int) and i >= n_pairs:
            return None
        return pairs_ref[0, i], pairs_ref[1, i], pairs_ref[2, i], False

    def aligned(start):
        return start if isinstance(start, int) else pl.multiple_of(start, cw)

    def keys_used(diagonal, c0):
        return c0 + cw if diagonal else t

    def scores(slot, pr, mp, c0):
        qi, j, kind, diagonal = pr
        cols, nk = slice(c0, c0 + cw), keys_used(diagonal, c0)
        kblk = k_ref[0, pl.ds(aligned(j * t), nk), :]
        qblk = q_sc[mp, :, pl.ds(aligned(qi * t + c0), cw)]
        s = _dot(kblk, qblk) + bias_sc[kind, 0:nk, cols]
        s_sc[slot, mp, 0:nk, cols] = s
        mx_sc[slot, mp, :, cols] = jnp.max(s, axis=0, keepdims=True)

    def softmax_pv(slot, pr, mp, c0):
        qi, j, _, diagonal = pr
        cols, nk = slice(c0, c0 + cw), keys_used(diagonal, c0)
        vblk = v_sc[:, pl.ds(aligned(j * t), nk)]
        m_old = m_sc[qi, mp, :, cols]
        m_new = jnp.maximum(m_old, mx_sc[slot, mp, :, cols])
        alpha = jnp.exp2(m_old - m_new)
        p = jnp.exp2((s_sc[slot, mp, 0:nk, cols] - m_new).astype(BF16))
        pv = _dot(vblk, p)
        l_sc[qi, mp, :, cols] = alpha * l_sc[qi, mp, :, cols] + pv[V_HEAD_DIM:V_HEAD_DIM + 1]
        acc_sc[qi, mp, :, cols] = alpha * acc_sc[qi, mp, :, cols] + pv[0:V_HEAD_DIM]
        m_sc[qi, mp, :, cols] = m_new

    n_pairs = pairs_ref.shape[1]
    pieces = [(mp, c0) for mp in range(2) for c0 in range(0, t, cw)]

    def stage(i, parity):
        for mp, c0 in pieces:
            if pair(i + 1) is not None:
                scores(1 - parity, pair(i + 1), mp, c0)
            softmax_pv(parity, pair(i), mp, c0)

    for mp, c0 in pieces:
        scores(0, pair(0), mp, c0)
    for i in range(nq):
        stage(i, i % 2)

    def trip_body(trip, carry):
        first = nq + ATTN_STAGES_PER_TRIP * trip
        for u in range(ATTN_STAGES_PER_TRIP):
            stage(first + u, (nq + u) % 2)
        return carry

    trips = (n_pairs - 1 - nq) // ATTN_STAGES_PER_TRIP
    lax.fori_loop(0, trips, trip_body, 0)
    for i in range(nq + trips * ATTN_STAGES_PER_TRIP, n_pairs):
        stage(i, i % 2)

    lv = lam_ref[...]
    lam = (jnp.exp(jnp.sum(lv[0:1] * lv[1:2], axis=-1, keepdims=True))
           - jnp.exp(jnp.sum(lv[2:3] * lv[3:4], axis=-1, keepdims=True)) + lambda_init)
    gain = g_ref[...] * (1.0 - lambda_init)
    for qi in range(nq):
        o = acc_sc[qi, 0] * (1.0 / l_sc[qi, 0]) - acc_sc[qi, 1] * (lam / l_sc[qi, 1])
        o = o * lax.rsqrt(jnp.mean(o * o, axis=0, keepdims=True) + EPS) * gain
        o_ref[0, qi * t:(qi + 1) * t, :] = o.T.astype(BF16)


def _attention(qt, k, vt, rel_bias, lam_vecs, g_subln, lambda_init):
    b, s, _ = k.shape
    t = ATTN_TILE
    assert s % t == 0 and t >= MAX_DISTANCE
    nq = s // t
    hd = V_HEAD_DIM
    pairs = _attn_pairs(nq)
    kern = functools.partial(_attn_kernel, t=t, nq=nq, thresholds=_bucket_thresholds(s),
                             lambda_init=lambda_init)
    return pl.pallas_call(
        kern,
        grid=(N_HEADS, b),
        in_specs=[
            pl.BlockSpec(memory_space=pltpu.SMEM),
            pl.BlockSpec(memory_space=pltpu.SMEM),
            pl.BlockSpec((1, hd, s), lambda h, bi: (bi, h, 0)),
            pl.BlockSpec((1, s, hd), lambda h, bi: (bi, 0, h)),
            pl.BlockSpec((1, hd, s), lambda h, bi: (bi, h, 0)),
            _resident(lam_vecs.shape),
            _resident((hd, 1)),
        ],
        out_specs=pl.BlockSpec((1, s, hd), lambda h, bi: (bi, 0, h)),
        out_shape=jax.ShapeDtypeStruct((b, s, N_HEADS * hd), BF16),
        scratch_shapes=[
            pltpu.VMEM((3, t, t), F32),
            pltpu.VMEM((2, hd, s), BF16),
            pltpu.VMEM((hd + ONES_ROWS, s), BF16),
            pltpu.VMEM((2, 2, t, t), F32),
            pltpu.VMEM((2, 2, 1, t), F32),
            pltpu.VMEM((nq, 2, 1, t), F32),
            pltpu.VMEM((nq, 2, 1, t), F32),
            pltpu.VMEM((nq, 2, hd, t), F32),
        ],
        compiler_params=pltpu.CompilerParams(
            dimension_semantics=("arbitrary", "arbitrary"),
            vmem_limit_bytes=V7X_VMEM_LIMIT),
        name="diff_attention",
    )(rel_bias, jnp.asarray(pairs), qt, k, vt, lam_vecs, g_subln)


def kernel(x, p, g_pre_mix, g_post_mix, g_pre_ffn, g_post_ffn, g_pre_ple, g_post_ple,
           w_sc_in, w_sc_conv, w_sc_out, g_kv, w_kv, rel_bias, w_q, diff_lambda,
           g_subln, w_o, w_ffn_up, w_ffn_conv, w_ffn_down, w_ple_gate, w_ple_proj):
    b, s, d = x.shape
    depth = p.shape[0]
    n_a = depth // 2
    qk_width = N_HEADS * 2 * HEAD_DIM
    x2 = x.reshape(b * s, d)
    p2 = p.reshape(depth, b * s, p.shape[-1])
    bf = lambda w: w.astype(BF16)
    assert n_a >= 1 and w_kv.shape == (d, 2 * qk_width) and qk_width == d

    later = dict(up=w_ffn_up, down=w_ffn_down, gate=w_ple_gate, proj=w_ple_proj, o=w_o, q=w_q, kv=w_kv[None])
    for i in range(depth):
        attn = None
        if i < n_a:
            x2, casted = _mixer(x2, g_pre_mix[i][None], g_post_mix[i][None], bf(w_sc_in[i]),
                                w_sc_conv[i], bf(w_sc_out[i]), s, list(later.values()) if i == 0 else [])
            if i == 0:
                wb = dict(zip(later, casted))
        else:
            j = i - n_a
            assert j == 0, "one attention layer: K/V and Q read the same stream"
            lambda_init = 0.8 - 0.6 * math.exp(-0.3 * i)
            qt, k, vt = _qkv(x2.reshape(b, s, d), g_pre_mix[i][None], g_kv[None], wb["q"][j], wb["kv"][0])
            attn = _attention(qt, k, vt, rel_bias, diff_lambda[j], g_subln[j][:, None],
                              lambda_init).reshape(b * s, d)
        gains = jnp.stack([g_pre_ffn[i], g_post_ffn[i], g_pre_ple[i], g_post_ple[i], g_post_mix[i]])
        x2 = _ffn_ple(x2, p2, i, gains, wb["up"], w_ffn_conv, wb["down"], wb["gate"], wb["proj"], s,
                      attn=attn, w_o=None if attn is None else wb["o"][i - n_a])
    return x2.reshape(b, s, d)
```

```python
import functools
import math

import numpy as np
import jax
import jax.numpy as jnp
from jax import lax
from jax.experimental import pallas as pl
from jax.experimental.pallas import tpu as pltpu

F32 = jnp.float32
BF16 = jnp.bfloat16

EPS = 1e-6
LOG2E = math.log2(math.e)
CONV_WIDTH = 3
N_HEADS = 8
HEAD_DIM = 64
V_HEAD_DIM = 2 * HEAD_DIM
N_BUCKETS = 32
MAX_DISTANCE = 128

V7X_SUBLANES = 8
V7X_MXU_WIDTH = 256
V7X_VMEM_LIMIT = 56 * 1024 * 1024

TOKEN_TILE = 512
SUB_TILE = 256
ATTN_TILE = 512
ATTN_STAGES_PER_TRIP = 12
HALO = V7X_SUBLANES
BF16_ROWS = 2 * V7X_SUBLANES
ONES_ROWS = BF16_ROWS


def _dot(a, b):
    return jnp.dot(a, b, preferred_element_type=F32)


def _dot_nt(a, b):
    return lax.dot_general(a, b, (((1,), (1,)), ((), ())), preferred_element_type=F32)


def _normalize(x):
    return x * lax.rsqrt(jnp.mean(x * x, axis=-1, keepdims=True) + EPS)


def _rms(x, g):
    return _normalize(x) * g


def _sigmoid(x):
    return 1.0 / (1.0 + jnp.exp2(x * -LOG2E))


def _causal_conv_cols(buf, y, wconv_ref, cols, r0):
    rows = y.shape[0]
    buf[HALO + r0:HALO + r0 + rows, cols] = y
    return (wconv_ref[0:1, cols] * buf[HALO + r0 - 2:HALO + r0 - 2 + rows, cols]
            + wconv_ref[1:2, cols] * buf[HALO + r0 - 1:HALO + r0 - 1 + rows, cols]
            + wconv_ref[2:3, cols] * y)


def _emit_staggered(chains):
    depth = max(len(chain) for chain in chains)
    for step in range(depth + len(chains) - 1):
        for lag, chain in enumerate(chains):
            if 0 <= step - lag < len(chain):
                chain[step - lag]()


def _alternate(a, b):
    out = []
    for i in range(max(len(a), len(b))):
        out += a[i:i + 1] + b[i:i + 1]
    return out


def _resident(shape, layer=None):
    if layer is None:
        return pl.BlockSpec(shape, lambda *_: (0,) * len(shape), pipeline_mode=pl.Buffered(1))
    return pl.BlockSpec((None,) + tuple(shape), lambda *_: (layer,) + (0,) * len(shape),
                        pipeline_mode=pl.Buffered(1))


def _cast_plan(w, n_steps):
    rows = w.shape[1]
    r = BF16_ROWS * pl.cdiv(pl.cdiv(rows, n_steps), BF16_ROWS)
    while rows % r:
        r += BF16_ROWS
    return r, rows // r


def _mixer_kernel(*refs, tm, tiles_per_seq, d, n_cast):
    x_ref, gpre_ref, gpost_ref, win_ref, wconv_ref, wout_ref = refs[:6]
    cast_in = refs[6:6 + n_cast]
    o_ref = refs[6 + n_cast]
    cast_out = refs[7 + n_cast:7 + 2 * n_cast]
    zbuf, mbuf = refs[7 + 2 * n_cast:]
    i = pl.program_id(0)

    @pl.when(lax.rem(i, tiles_per_seq) == 0)
    def _():
        zbuf[0:HALO, :] = jnp.zeros((HALO, d), F32)

    def cast(src, dst):
        dst[...] = src[...].astype(BF16)

    casts = [functools.partial(cast, src, dst) for src, dst in zip(cast_in, cast_out)]

    cw = V7X_MXU_WIDTH

    def chain(r0):
        rows = slice(r0, r0 + SUB_TILE)
        v = {}

        def pre():
            v["h"] = _rms(x_ref[rows, :], gpre_ref[...]).astype(BF16)

        def chunk(c):
            cols = slice(c * cw, (c + 1) * cw)
            gb = _dot(v["h"], win_ref[:, c * cw:(c + 1) * cw])
            gc = _dot(v["h"], win_ref[:, d + c * cw:d + (c + 1) * cw])
            u = _dot(v["h"], win_ref[:, 2 * d + c * cw:2 * d + (c + 1) * cw])
            conv = _causal_conv_cols(zbuf, gc * u, wconv_ref, cols, r0)
            mbuf[rows, cols] = (gb * conv).astype(BF16)

        def out_proj(k):
            v.setdefault("mix", []).append(_dot(mbuf[rows, :], wout_ref[:, k * cw:(k + 1) * cw]))

        def post():
            mix = jnp.concatenate(v.pop("mix"), axis=1)
            o_ref[rows, :] = x_ref[rows, :] + _rms(mix, gpost_ref[...])

        front = [pre] + [functools.partial(chunk, c) for c in range(d // cw)]
        return front, [functools.partial(out_proj, k) for k in range(d // cw)], post

    chains = [chain(r0) for r0 in range(0, tm, SUB_TILE)]
    _emit_staggered([front for front, _, _ in chains] + [casts])
    (_, mm_a, post_a), (_, mm_b, post_b) = chains
    for piece in mm_a + _alternate(mm_b, [post_a]) + [post_b]:
        piece()
    zbuf[0:HALO, :] = zbuf[tm:tm + HALO, :]


def _mixer(x2, g_pre, g_post, w_in, w_conv, w_out, seq, to_cast):
    t, d = x2.shape
    tm = TOKEN_TILE
    n_steps = t // tm
    plans = [_cast_plan(w, n_steps) for w in to_cast]
    kern = functools.partial(_mixer_kernel, tm=tm, tiles_per_seq=seq // tm, d=d, n_cast=len(plans))
    row = pl.BlockSpec((tm, d), lambda i: (i, 0))
    cast_specs = [pl.BlockSpec((w.shape[0], r, w.shape[2]), lambda i, nb=nb: (0, jnp.minimum(i, nb - 1), 0))
                  for w, (r, nb) in zip(to_cast, plans)]
    outs = pl.pallas_call(
        kern,
        grid=(n_steps,),
        in_specs=[row, _resident((1, d)), _resident((1, d)), _resident((d, 3 * d)),
                  _resident((CONV_WIDTH, d)), _resident((d, d))] + cast_specs,
        out_specs=[row] + cast_specs,
        out_shape=[jax.ShapeDtypeStruct((t, d), F32)] + [jax.ShapeDtypeStruct(w.shape, BF16) for w in to_cast],
        scratch_shapes=[pltpu.VMEM((tm + HALO, d), F32), pltpu.VMEM((tm, d), BF16)],
        compiler_params=pltpu.CompilerParams(
            dimension_semantics=("arbitrary",), vmem_limit_bytes=V7X_VMEM_LIMIT),
        name="sconv_mixer",
    )(x2, g_pre, g_post, w_in, w_conv, w_out, *to_cast)
    return outs[0], outs[1:]


def _ffn_ple_kernel(*refs, tm, tiles_per_seq, d, f, with_attn_out):
    if with_attn_out:
        (x_ref, p_ref, a_ref, gains_ref, wo_ref, wup_ref, wconv_ref, wdown_ref, wgate_ref,
         wproj_ref, o_ref, ybuf, abuf) = refs
    else:
        (x_ref, p_ref, gains_ref, wup_ref, wconv_ref, wdown_ref, wgate_ref,
         wproj_ref, o_ref, ybuf, abuf) = refs
    i = pl.program_id(0)

    @pl.when(lax.rem(i, tiles_per_seq) == 0)
    def _():
        ybuf[0:HALO, :] = jnp.zeros((HALO, 2 * f), F32)

    cw = V7X_MXU_WIDTH

    def chain(r0):
        rows = slice(r0, r0 + SUB_TILE)
        v = {}

        def pre():
            x = x_ref[rows, :]
            if with_attn_out:
                x = x + _rms(jnp.concatenate(v.pop("mix"), axis=1), gains_ref[4:5, :])
            v["x"] = x
            v["hn"] = _rms(x, gains_ref[0:1, :]).astype(BF16)

        def attn_out(k):
            v.setdefault("mix", []).append(_dot(a_ref[rows, :], wo_ref[:, k * cw:(k + 1) * cw]))

        def chunk(c):
            gcols = slice(c * cw, (c + 1) * cw)
            ucols = slice(f + c * cw, f + (c + 1) * cw)
            gate = _causal_conv_cols(ybuf, _dot(v["hn"], wup_ref[:, c * cw:(c + 1) * cw]),
                                     wconv_ref, gcols, r0)
            up = _causal_conv_cols(ybuf, _dot(v["hn"], wup_ref[:, f + c * cw:f + (c + 1) * cw]),
                                   wconv_ref, ucols, r0)
            abuf[rows, gcols] = (gate * _sigmoid(gate) * up).astype(BF16)

        def down(k):
            v.setdefault("ffn", []).append(_dot(abuf[rows, :], wdown_ref[:, k * cw:(k + 1) * cw]))

        def ffn_norm():
            v["x"] = v["x"] + _rms(jnp.concatenate(v.pop("ffn"), axis=1), gains_ref[1:2, :])

        def embed_norm():
            v["hg"] = _rms(v["x"], gains_ref[2:3, :]).astype(BF16)

        def gate(k):
            v.setdefault("gate", []).append(_dot(v["hg"], wgate_ref[:, k * cw:(k + 1) * cw]))

        def proj():
            v["emb"] = _dot(p_ref[rows, :].astype(BF16), wproj_ref[...])

        def gated():
            v["ple"] = _sigmoid(jnp.concatenate(v.pop("gate"), axis=1)) * v.pop("emb")

        def store():
            o_ref[rows, :] = v["x"] + _rms(v.pop("ple"), gains_ref[3:4, :])

        nk = d // cw
        front = [pre] + [functools.partial(chunk, c) for c in range(f // cw)]
        matmuls = ([functools.partial(down, k) for k in range(nk)]
                   + [functools.partial(gate, k) for k in range(nk)] + [proj])
        elementwise = [ffn_norm, embed_norm, gated, store]
        opening = [functools.partial(attn_out, k) for k in range(nk)] if with_attn_out else []
        return opening, front, matmuls, elementwise, nk

    (open_a, front_a, mm_a, ew_a, nk), (open_b, front_b, mm_b, ew_b, _) = [
        chain(r0) for r0 in range(0, tm, SUB_TILE)]
    order = open_a + _alternate(open_b, front_a[:1]) + _alternate(front_a[1:], front_b)
    order += mm_a[:nk]
    order += _alternate(mm_b[:nk], ew_a[:2])
    order += _alternate(mm_a[nk:], ew_b[:2])
    order += _alternate(mm_b[nk:], ew_a[2:])
    order += ew_b[2:]
    for piece in order:
        piece()
    ybuf[0:HALO, :] = ybuf[tm:tm + HALO, :]


def _ffn_ple(x2, p3, layer, gains, w_up, w_conv, w_down, w_gate, w_proj, seq, attn=None, w_o=None):
    t, d = x2.shape
    f = w_down.shape[1]
    pd = p3.shape[2]
    tm = TOKEN_TILE
    with_attn_out = attn is not None
    kern = functools.partial(_ffn_ple_kernel, tm=tm, tiles_per_seq=seq // tm, d=d, f=f,
                             with_attn_out=with_attn_out)
    row = pl.BlockSpec((tm, d), lambda i: (i, 0))
    prow = pl.BlockSpec((None, tm, pd), lambda i: (layer, i, 0))
    args = [x2, p3]
    specs = [row, prow]
    if with_attn_out:
        args.append(attn)
        specs.append(row)
    args.append(gains)
    specs.append(_resident(gains.shape))
    if with_attn_out:
        args.append(w_o)
        specs.append(_resident((d, d)))
    args += [w_up, w_conv, w_down, w_gate, w_proj]
    specs += [_resident((d, 2 * f), layer), _resident((CONV_WIDTH, 2 * f), layer), _resident((f, d), layer),
              _resident((d, d), layer), _resident((pd, d), layer)]
    return pl.pallas_call(
        kern,
        grid=(t // tm,),
        in_specs=specs,
        out_specs=row,
        out_shape=jax.ShapeDtypeStruct((t, d), F32),
        scratch_shapes=[pltpu.VMEM((tm + HALO, 2 * f), F32), pltpu.VMEM((tm, f), BF16)],
        compiler_params=pltpu.CompilerParams(
            dimension_semantics=("arbitrary",), vmem_limit_bytes=V7X_VMEM_LIMIT),
        name="attn_out_ffn_ple" if with_attn_out else "ffn_ple",
    )(*args)


def _qkv_kernel(x_ref, gq_ref, gkv_ref, wq_ref, wk_ref, wv_ref, qt_ref, k_ref, vt_ref, wqt_ref, wvt_ref):
    @pl.when((pl.program_id(0) == 0) & (pl.program_id(1) == 0))
    def _():
        wqt_ref[...] = wq_ref[...].T
        wvt_ref[...] = wv_ref[...].T

    def chain(r0):
        rows = slice(r0, r0 + SUB_TILE)
        v = {}

        def pre():
            xn = _normalize(x_ref[0, rows, :])
            v["h"] = (xn * gq_ref[...]).astype(BF16)
            v["hk"] = (xn * gkv_ref[...]).astype(BF16)

        def q_proj():
            qt_ref[0, :, rows] = (_dot_nt(wqt_ref[...], v["h"]) * (HEAD_DIM ** -0.5 * LOG2E)).astype(BF16)

        def k_proj():
            k_ref[0, rows, :] = _dot(v["hk"], wk_ref[...]).astype(BF16)

        def v_proj():
            vt_ref[0, :, rows] = _dot_nt(wvt_ref[...], v["hk"]).astype(BF16)

        return [pre, q_proj, k_proj, v_proj]

    _emit_staggered([chain(r0) for r0 in range(0, x_ref.shape[1], SUB_TILE)])


def _qkv(x3, g_q, g_kv, w_q, w_kv):
    b, s, d = x3.shape
    tm = TOKEN_TILE
    row = pl.BlockSpec((1, tm, d), lambda bi, i: (bi, i, 0))
    col = pl.BlockSpec((1, d, tm), lambda bi, i: (bi, 0, i))
    w_k_spec = pl.BlockSpec((d, d), lambda *_: (0, 0), pipeline_mode=pl.Buffered(1))
    w_v_spec = pl.BlockSpec((d, d), lambda *_: (0, 1), pipeline_mode=pl.Buffered(1))
    return pl.pallas_call(
        _qkv_kernel,
        grid=(b, s // tm),
        in_specs=[row, _resident((1, d)), _resident((1, d)), _resident((d, d)), w_k_spec, w_v_spec],
        out_specs=[col, row, col],
        out_shape=[jax.ShapeDtypeStruct((b, d, s), BF16), jax.ShapeDtypeStruct((b, s, d), BF16),
                   jax.ShapeDtypeStruct((b, d, s), BF16)],
        scratch_shapes=[pltpu.VMEM((d, d), BF16), pltpu.VMEM((d, d), BF16)],
        compiler_params=pltpu.CompilerParams(
            dimension_semantics=("arbitrary", "arbitrary"), vmem_limit_bytes=V7X_VMEM_LIMIT),
        name="qkv_proj",
    )(x3, g_q, g_kv, w_q, w_kv, w_kv)


def _bucket_thresholds(seq):
    dist = np.arange(seq)
    max_exact = N_BUCKETS // 2
    large = max_exact + (np.log(np.maximum(dist, 1).astype(np.float32) / np.float32(max_exact))
                         / np.float32(math.log(MAX_DISTANCE / max_exact))
                         * np.float32(N_BUCKETS - max_exact)).astype(np.int32)
    bucket = np.where(dist < max_exact, dist, np.minimum(large, N_BUCKETS - 1))
    assert np.all(np.diff(bucket) >= 0)
    thr = [int(np.argmax(bucket >= b)) if np.any(bucket >= b) else seq for b in range(N_BUCKETS)]
    assert np.all(bucket[MAX_DISTANCE:] == N_BUCKETS - 1)
    return thr


def _attn_pairs(nq):
    pairs = [(qi, qi, 0) for qi in range(nq)]
    pairs += [(qi, qi - n, min(n, 2)) for qi in range(nq) for n in range(1, qi + 1)]
    return np.asarray(pairs, np.int32).T


def _attn_kernel(rb_ref, pairs_ref, qt_ref, k_ref, vt_ref, lam_ref, g_ref, o_ref,
                 bias_sc, q_sc, v_sc, s_sc, mx_sc, m_sc, l_sc, acc_sc,
                 *, t, nq, thresholds, lambda_init):
    h = pl.program_id(0)
    bi = pl.program_id(1)

    @pl.when(bi == 0)
    def _():
        row = lax.broadcasted_iota(jnp.int32, (t, t), 0)
        col = lax.broadcasted_iota(jnp.int32, (t, t), 1)
        for tile, delta in ((0, 0), (1, t)):
            dist = col - row + delta
            val = jnp.full((t, t), rb_ref[0, h], F32)
            for b in range(1, N_BUCKETS):
                val = jnp.where(dist >= thresholds[b], rb_ref[b, h], val)
            val = val * LOG2E
            if tile == 0:
                val = jnp.where(dist >= 0, val, -jnp.inf)
            bias_sc[tile] = val
        bias_sc[2] = jnp.full((t, t), rb_ref[N_BUCKETS - 1, h] * LOG2E, F32)
        zero = jnp.zeros((HEAD_DIM, nq * t), BF16)
        q_sc[0, HEAD_DIM:, :] = zero
        q_sc[1, 0:HEAD_DIM, :] = zero
        v_sc[V_HEAD_DIM:, :] = jnp.ones((ONES_ROWS, nq * t), BF16)
        l_sc[...] = jnp.zeros(l_sc.shape, F32)
        acc_sc[...] = jnp.zeros(acc_sc.shape, F32)

    q_sc[0, 0:HEAD_DIM, :] = qt_ref[0, 0:HEAD_DIM, :]
    q_sc[1, HEAD_DIM:, :] = qt_ref[0, HEAD_DIM:, :]
    v_sc[0:V_HEAD_DIM, :] = vt_ref[0]
    m_sc[...] = jnp.full(m_sc.shape, -jnp.inf, F32)

    cw = V7X_MXU_WIDTH

    def pair(i):
        if isinstance(i, int) and i < nq:
            return i, i, 0, True
        if isinstance(i, int) and i >= n_pairs:
            return None
        return pairs_ref[0, i], pairs_ref[1, i], pairs_ref[2, i], False

    def aligned(start):
        return start if isinstance(start, int) else pl.multiple_of(start, cw)

    def keys_used(diagonal, c0):
        return c0 + cw if diagonal else t

    def scores(slot, pr, mp, c0):
        qi, j, kind, diagonal = pr
        cols, nk = slice(c0, c0 + cw), keys_used(diagonal, c0)
        kblk = k_ref[0, pl.ds(aligned(j * t), nk), :]
        qblk = q_sc[mp, :, pl.ds(aligned(qi * t + c0), cw)]
        s = _dot(kblk, qblk) + bias_sc[kind, 0:nk, cols]
        s_sc[slot, mp, 0:nk, cols] = s
        mx_sc[slot, mp, :, cols] = jnp.max(s, axis=0, keepdims=True)

    def softmax_pv(slot, pr, mp, c0):
        qi, j, _, diagonal = pr
        cols, nk = slice(c0, c0 + cw), keys_used(diagonal, c0)
        vblk = v_sc[:, pl.ds(aligned(j * t), nk)]
        m_old = m_sc[qi, mp, :, cols]
        m_new = jnp.maximum(m_old, mx_sc[slot, mp, :, cols])
        alpha = jnp.exp2(m_old - m_new)
        p = jnp.exp2((s_sc[slot, mp, 0:nk, cols] - m_new).astype(BF16))
        pv = _dot(vblk, p)
        l_sc[qi, mp, :, cols] = alpha * l_sc[qi, mp, :, cols] + pv[V_HEAD_DIM:V_HEAD_DIM + 1]
        acc_sc[qi, mp, :, cols] = alpha * acc_sc[qi, mp, :, cols] + pv[0:V_HEAD_DIM]
        m_sc[qi, mp, :, cols] = m_new

    n_pairs = pairs_ref.shape[1]
    pieces = [(mp, c0) for mp in range(2) for c0 in range(0, t, cw)]

    def stage(i, parity):
        for mp, c0 in pieces:
            if pair(i + 1) is not None:
                scores(1 - parity, pair(i + 1), mp, c0)
            softmax_pv(parity, pair(i), mp, c0)

    for mp, c0 in pieces:
        scores(0, pair(0), mp, c0)
    for i in range(nq):
        stage(i, i % 2)

    def trip_body(trip, carry):
        first = nq + ATTN_STAGES_PER_TRIP * trip
        for u in range(ATTN_STAGES_PER_TRIP):
            stage(first + u, (nq + u) % 2)
        return carry

    trips = (n_pairs - 1 - nq) // ATTN_STAGES_PER_TRIP
    lax.fori_loop(0, trips, trip_body, 0)
    for i in range(nq + trips * ATTN_STAGES_PER_TRIP, n_pairs):
        stage(i, i % 2)

    lv = lam_ref[...]
    lam = (jnp.exp(jnp.sum(lv[0:1] * lv[1:2], axis=-1, keepdims=True))
           - jnp.exp(jnp.sum(lv[2:3] * lv[3:4], axis=-1, keepdims=True)) + lambda_init)
    gain = g_ref[...] * (1.0 - lambda_init)
    for qi in range(nq):
        o = acc_sc[qi, 0] * (1.0 / l_sc[qi, 0]) - acc_sc[qi, 1] * (lam / l_sc[qi, 1])
        o = o * lax.rsqrt(jnp.mean(o * o, axis=0, keepdims=True) + EPS) * gain
        o_ref[0, qi * t:(qi + 1) * t, :] = o.T.astype(BF16)


def _attention(qt, k, vt, rel_bias, lam_vecs, g_subln, lambda_init):
    b, s, _ = k.shape
    t = ATTN_TILE
    assert s % t == 0 and t >= MAX_DISTANCE
    nq = s // t
    hd = V_HEAD_DIM
    pairs = _attn_pairs(nq)
    kern = functools.partial(_attn_kernel, t=t, nq=nq, thresholds=_bucket_thresholds(s),
                             lambda_init=lambda_init)
    return pl.pallas_call(
        kern,
        grid=(N_HEADS, b),
        in_specs=[
            pl.BlockSpec(memory_space=pltpu.SMEM),
            pl.BlockSpec(memory_space=pltpu.SMEM),
            pl.BlockSpec((1, hd, s), lambda h, bi: (bi, h, 0)),
            pl.BlockSpec((1, s, hd), lambda h, bi: (bi, 0, h)),
            pl.BlockSpec((1, hd, s), lambda h, bi: (bi, h, 0)),
            _resident(lam_vecs.shape),
            _resident((hd, 1)),
        ],
        out_specs=pl.BlockSpec((1, s, hd), lambda h, bi: (bi, 0, h)),
        out_shape=jax.ShapeDtypeStruct((b, s, N_HEADS * hd), BF16),
        scratch_shapes=[
            pltpu.VMEM((3, t, t), F32),
            pltpu.VMEM((2, hd, s), BF16),
            pltpu.VMEM((hd + ONES_ROWS, s), BF16),
            pltpu.VMEM((2, 2, t, t), F32),
            pltpu.VMEM((2, 2, 1, t), F32),
            pltpu.VMEM((nq, 2, 1, t), F32),
            pltpu.VMEM((nq, 2, 1, t), F32),
            pltpu.VMEM((nq, 2, hd, t), F32),
        ],
        compiler_params=pltpu.CompilerParams(
            dimension_semantics=("arbitrary", "arbitrary"),
            vmem_limit_bytes=V7X_VMEM_LIMIT),
        name="diff_attention",
    )(rel_bias, jnp.asarray(pairs), qt, k, vt, lam_vecs, g_subln)


def kernel(x, p, g_pre_mix, g_post_mix, g_pre_ffn, g_post_ffn, g_pre_ple, g_post_ple,
           w_sc_in, w_sc_conv, w_sc_out, g_kv, w_kv, rel_bias, w_q, diff_lambda,
           g_subln, w_o, w_ffn_up, w_ffn_conv, w_ffn_down, w_ple_gate, w_ple_proj):
    b, s, d = x.shape
    depth = p.shape[0]
    n_a = depth // 2
    qk_width = N_HEADS * 2 * HEAD_DIM
    x2 = x.reshape(b * s, d)
    p2 = p.reshape(depth, b * s, p.shape[-1])
    bf = lambda w: w.astype(BF16)
    assert n_a >= 1 and w_kv.shape == (d, 2 * qk_width) and qk_width == d

    later = dict(up=w_ffn_up, down=w_ffn_down, gate=w_ple_gate, proj=w_ple_proj, o=w_o, q=w_q, kv=w_kv[None])
    for i in range(depth):
        attn = None
        if i < n_a:
            x2, casted = _mixer(x2, g_pre_mix[i][None], g_post_mix[i][None], bf(w_sc_in[i]),
                                w_sc_conv[i], bf(w_sc_out[i]), s, list(later.values()) if i == 0 else [])
            if i == 0:
                wb = dict(zip(later, casted))
        else:
            j = i - n_a
            assert j == 0, "one attention layer: K/V and Q read the same stream"
            lambda_init = 0.8 - 0.6 * math.exp(-0.3 * i)
            qt, k, vt = _qkv(x2.reshape(b, s, d), g_pre_mix[i][None], g_kv[None], wb["q"][j], wb["kv"][0])
            attn = _attention(qt, k, vt, rel_bias, diff_lambda[j], g_subln[j][:, None],
                              lambda_init).reshape(b * s, d)
        gains = jnp.stack([g_pre_ffn[i], g_post_ffn[i], g_pre_ple[i], g_post_ple[i], g_post_mix[i]])
        x2 = _ffn_ple(x2, p2, i, gains, wb["up"], w_ffn_conv, wb["down"], wb["gate"], wb["proj"], s,
                      attn=attn, w_o=None if attn is None else wb["o"][i - n_a])
    return x2.reshape(b, s, d)
```

```python
import functools
import math

import numpy as np
import jax
import jax.numpy as jnp
from jax import lax
from jax.experimental import pallas as pl
from jax.experimental.pallas import tpu as pltpu

F32 = jnp.float32
BF16 = jnp.bfloat16

EPS = 1e-6
LOG2E = math.log2(math.e)
CONV_WIDTH = 3
N_HEADS = 8
HEAD_DIM = 64
V_HEAD_DIM = 2 * HEAD_DIM
N_BUCKETS = 32
MAX_DISTANCE = 128

V7X_SUBLANES = 8
V7X_MXU_WIDTH = 256
V7X_VMEM_LIMIT = 56 * 1024 * 1024

TOKEN_TILE = 512
QKV_TILE = 1024
SUB_TILE = 256
ATTN_TILE = 512
ATTN_STAGES_PER_TRIP = 12
HALO = V7X_SUBLANES
BF16_ROWS = 2 * V7X_SUBLANES
ONES_ROWS = BF16_ROWS


def _dot(a, b):
    return jnp.dot(a, b, preferred_element_type=F32)


def _dot_nt(a, b):
    return lax.dot_general(a, b, (((1,), (1,)), ((), ())), preferred_element_type=F32)


def _normalize(x):
    return x * lax.rsqrt(jnp.mean(x * x, axis=-1, keepdims=True) + EPS)


def _rms(x, g):
    return _normalize(x) * g


def _sigmoid(x):
    return 1.0 / (1.0 + jnp.exp2(x * -LOG2E))


def _causal_conv_cols(buf, y, wconv_ref, cols, r0):
    rows = y.shape[0]
    buf[HALO + r0:HALO + r0 + rows, cols] = y
    return (wconv_ref[0:1, cols] * buf[HALO + r0 - 2:HALO + r0 - 2 + rows, cols]
            + wconv_ref[1:2, cols] * buf[HALO + r0 - 1:HALO + r0 - 1 + rows, cols]
            + wconv_ref[2:3, cols] * y)


def _emit_staggered(chains):
    depth = max(len(chain) for chain in chains)
    for step in range(depth + len(chains) - 1):
        for lag, chain in enumerate(chains):
            if 0 <= step - lag < len(chain):
                chain[step - lag]()


def _alternate(a, b):
    out = []
    for i in range(max(len(a), len(b))):
        out += a[i:i + 1] + b[i:i + 1]
    return out


def _resident(shape, layer=None):
    if layer is None:
        return pl.BlockSpec(shape, lambda *_: (0,) * len(shape), pipeline_mode=pl.Buffered(1))
    return pl.BlockSpec((None,) + tuple(shape), lambda *_: (layer,) + (0,) * len(shape),
                        pipeline_mode=pl.Buffered(1))


def _cast_plan(w, n_steps):
    rows = w.shape[1]
    r = BF16_ROWS * pl.cdiv(pl.cdiv(rows, n_steps), BF16_ROWS)
    while rows % r:
        r += BF16_ROWS
    return r, rows // r


def _mixer_kernel(*refs, tm, tiles_per_seq, d, n_cast):
    x_ref, gpre_ref, gpost_ref, win_ref, wconv_ref, wout_ref = refs[:6]
    cast_in = refs[6:6 + n_cast]
    o_ref = refs[6 + n_cast]
    cast_out = refs[7 + n_cast:7 + 2 * n_cast]
    zbuf, mbuf = refs[7 + 2 * n_cast:]
    i = pl.program_id(0)

    @pl.when(lax.rem(i, tiles_per_seq) == 0)
    def _():
        zbuf[0:HALO, :] = jnp.zeros((HALO, d), F32)

    def cast(src, dst):
        dst[...] = src[...].astype(BF16)

    casts = [functools.partial(cast, src, dst) for src, dst in zip(cast_in, cast_out)]

    cw = V7X_MXU_WIDTH

    def chain(r0):
        rows = slice(r0, r0 + SUB_TILE)
        v = {}

        def pre():
            v["h"] = _rms(x_ref[rows, :], gpre_ref[...]).astype(BF16)

        def chunk(c):
            cols = slice(c * cw, (c + 1) * cw)
            gb = _dot(v["h"], win_ref[:, c * cw:(c + 1) * cw])
            gc = _dot(v["h"], win_ref[:, d + c * cw:d + (c + 1) * cw])
            u = _dot(v["h"], win_ref[:, 2 * d + c * cw:2 * d + (c + 1) * cw])
            conv = _causal_conv_cols(zbuf, gc * u, wconv_ref, cols, r0)
            mbuf[rows, cols] = (gb * conv).astype(BF16)

        def out_proj(k):
            v.setdefault("mix", []).append(_dot(mbuf[rows, :], wout_ref[:, k * cw:(k + 1) * cw]))

        def post():
            mix = jnp.concatenate(v.pop("mix"), axis=1)
            o_ref[rows, :] = x_ref[rows, :] + _rms(mix, gpost_ref[...])

        front = [pre] + [functools.partial(chunk, c) for c in range(d // cw)]
        return front, [functools.partial(out_proj, k) for k in range(d // cw)], post

    chains = [chain(r0) for r0 in range(0, tm, SUB_TILE)]
    _emit_staggered([front for front, _, _ in chains] + [casts])
    (_, mm_a, post_a), (_, mm_b, post_b) = chains
    for piece in mm_a + _alternate(mm_b, [post_a]) + [post_b]:
        piece()
    zbuf[0:HALO, :] = zbuf[tm:tm + HALO, :]


def _mixer(x2, g_pre, g_post, w_in, w_conv, w_out, seq, to_cast):
    t, d = x2.shape
    tm = TOKEN_TILE
    n_steps = t // tm
    plans = [_cast_plan(w, n_steps) for w in to_cast]
    kern = functools.partial(_mixer_kernel, tm=tm, tiles_per_seq=seq // tm, d=d, n_cast=len(plans))
    row = pl.BlockSpec((tm, d), lambda i: (i, 0))
    cast_specs = [pl.BlockSpec((w.shape[0], r, w.shape[2]), lambda i, nb=nb: (0, jnp.minimum(i, nb - 1), 0))
                  for w, (r, nb) in zip(to_cast, plans)]
    outs = pl.pallas_call(
        kern,
        grid=(n_steps,),
        in_specs=[row, _resident((1, d)), _resident((1, d)), _resident((d, 3 * d)),
                  _resident((CONV_WIDTH, d)), _resident((d, d))] + cast_specs,
        out_specs=[row] + cast_specs,
        out_shape=[jax.ShapeDtypeStruct((t, d), F32)] + [jax.ShapeDtypeStruct(w.shape, BF16) for w in to_cast],
        scratch_shapes=[pltpu.VMEM((tm + HALO, d), F32), pltpu.VMEM((tm, d), BF16)],
        compiler_params=pltpu.CompilerParams(
            dimension_semantics=("arbitrary",), vmem_limit_bytes=V7X_VMEM_LIMIT),
        name="sconv_mixer",
    )(x2, g_pre, g_post, w_in, w_conv, w_out, *to_cast)
    return outs[0], outs[1:]


def _ffn_ple_kernel(*refs, tm, tiles_per_seq, d, f, with_attn_out):
    if with_attn_out:
        (x_ref, p_ref, a_ref, gains_ref, wo_ref, wup_ref, wconv_ref, wdown_ref, wgate_ref,
         wproj_ref, o_ref, ybuf, abuf) = refs
    else:
        (x_ref, p_ref, gains_ref, wup_ref, wconv_ref, wdown_ref, wgate_ref,
         wproj_ref, o_ref, ybuf, abuf) = refs
    i = pl.program_id(0)

    @pl.when(lax.rem(i, tiles_per_seq) == 0)
    def _():
        ybuf[0:HALO, :] = jnp.zeros((HALO, 2 * f), F32)

    cw = V7X_MXU_WIDTH

    def chain(r0):
        rows = slice(r0, r0 + SUB_TILE)
        v = {}

        def pre():
            x = x_ref[rows, :]
            if with_attn_out:
                x = x + _rms(jnp.concatenate(v.pop("mix"), axis=1), gains_ref[4:5, :])
            v["x"] = x
            v["hn"] = _rms(x, gains_ref[0:1, :]).astype(BF16)

        def attn_out(k):
            v.setdefault("mix", []).append(_dot(a_ref[rows, :], wo_ref[:, k * cw:(k + 1) * cw]))

        def chunk(c):
            gcols = slice(c * cw, (c + 1) * cw)
            ucols = slice(f + c * cw, f + (c + 1) * cw)
            gate = _causal_conv_cols(ybuf, _dot(v["hn"], wup_ref[:, c * cw:(c + 1) * cw]),
                                     wconv_ref, gcols, r0)
            up = _causal_conv_cols(ybuf, _dot(v["hn"], wup_ref[:, f + c * cw:f + (c + 1) * cw]),
                                   wconv_ref, ucols, r0)
            abuf[rows, gcols] = (gate * _sigmoid(gate) * up).astype(BF16)

        def down(k):
            v.setdefault("ffn", []).append(_dot(abuf[rows, :], wdown_ref[:, k * cw:(k + 1) * cw]))

        def ffn_norm():
            v["x"] = v["x"] + _rms(jnp.concatenate(v.pop("ffn"), axis=1), gains_ref[1:2, :])

        def embed_norm():
            v["hg"] = _rms(v["x"], gains_ref[2:3, :]).astype(BF16)

        def gate(k):
            v.setdefault("gate", []).append(_dot(v["hg"], wgate_ref[:, k * cw:(k + 1) * cw]))

        def proj():
            v["emb"] = _dot(p_ref[rows, :].astype(BF16), wproj_ref[...])

        def gated():
            v["ple"] = _sigmoid(jnp.concatenate(v.pop("gate"), axis=1)) * v.pop("emb")

        def store():
            o_ref[rows, :] = v["x"] + _rms(v.pop("ple"), gains_ref[3:4, :])

        nk = d // cw
        front = [pre] + [functools.partial(chunk, c) for c in range(f // cw)]
        matmuls = ([functools.partial(down, k) for k in range(nk)]
                   + [functools.partial(gate, k) for k in range(nk)] + [proj])
        elementwise = [ffn_norm, embed_norm, gated, store]
        opening = [functools.partial(attn_out, k) for k in range(nk)] if with_attn_out else []
        return opening, front, matmuls, elementwise, nk

    (open_a, front_a, mm_a, ew_a, nk), (open_b, front_b, mm_b, ew_b, _) = [
        chain(r0) for r0 in range(0, tm, SUB_TILE)]
    order = open_a + _alternate(open_b, front_a[:1]) + _alternate(front_a[1:], front_b)
    order += mm_a[:nk]
    order += _alternate(mm_b[:nk], ew_a[:2])
    order += _alternate(mm_a[nk:], ew_b[:2])
    order += _alternate(mm_b[nk:], ew_a[2:])
    order += ew_b[2:]
    for piece in order:
        piece()
    ybuf[0:HALO, :] = ybuf[tm:tm + HALO, :]


def _ffn_ple(x2, p3, layer, gains, w_up, w_conv, w_down, w_gate, w_proj, seq, attn=None, w_o=None):
    t, d = x2.shape
    f = w_down.shape[1]
    pd = p3.shape[2]
    tm = TOKEN_TILE
    with_attn_out = attn is not None
    kern = functools.partial(_ffn_ple_kernel, tm=tm, tiles_per_seq=seq // tm, d=d, f=f,
                             with_attn_out=with_attn_out)
    row = pl.BlockSpec((tm, d), lambda i: (i, 0))
    prow = pl.BlockSpec((None, tm, pd), lambda i: (layer, i, 0))
    args = [x2, p3]
    specs = [row, prow]
    if with_attn_out:
        args.append(attn)
        specs.append(row)
    args.append(gains)
    specs.append(_resident(gains.shape))
    if with_attn_out:
        args.append(w_o)
        specs.append(_resident((d, d)))
    args += [w_up, w_conv, w_down, w_gate, w_proj]
    specs += [_resident((d, 2 * f), layer), _resident((CONV_WIDTH, 2 * f), layer), _resident((f, d), layer),
              _resident((d, d), layer), _resident((pd, d), layer)]
    return pl.pallas_call(
        kern,
        grid=(t // tm,),
        in_specs=specs,
        out_specs=row,
        out_shape=jax.ShapeDtypeStruct((t, d), F32),
        scratch_shapes=[pltpu.VMEM((tm + HALO, 2 * f), F32), pltpu.VMEM((tm, f), BF16)],
        compiler_params=pltpu.CompilerParams(
            dimension_semantics=("arbitrary",), vmem_limit_bytes=V7X_VMEM_LIMIT),
        name="attn_out_ffn_ple" if with_attn_out else "ffn_ple",
    )(*args)


def _qkv_kernel(x_ref, gq_ref, gkv_ref, wq_ref, wk_ref, wv_ref, qt_ref, k_ref, vt_ref, wqt_ref, wvt_ref):
    @pl.when((pl.program_id(0) == 0) & (pl.program_id(1) == 0))
    def _():
        wqt_ref[...] = wq_ref[...].T
        wvt_ref[...] = wv_ref[...].T

    def chain(r0):
        rows = slice(r0, r0 + SUB_TILE)
        v = {}

        def pre():
            xn = _normalize(x_ref[0, rows, :])
            v["h"] = (xn * gq_ref[...]).astype(BF16)
            v["hk"] = (xn * gkv_ref[...]).astype(BF16)

        def q_proj():
            qt_ref[0, :, rows] = (_dot_nt(wqt_ref[...], v["h"]) * (HEAD_DIM ** -0.5 * LOG2E)).astype(BF16)

        def k_proj():
            k_ref[0, rows, :] = _dot(v["hk"], wk_ref[...]).astype(BF16)

        def v_proj():
            vt_ref[0, :, rows] = _dot_nt(wvt_ref[...], v["hk"]).astype(BF16)

        return [pre, q_proj, k_proj, v_proj]

    _emit_staggered([chain(r0) for r0 in range(0, x_ref.shape[1], SUB_TILE)])


def _qkv(x3, g_q, g_kv, w_q, w_kv):
    b, s, d = x3.shape
    tm = QKV_TILE
    row = pl.BlockSpec((1, tm, d), lambda bi, i: (bi, i, 0))
    col = pl.BlockSpec((1, d, tm), lambda bi, i: (bi, 0, i))
    w_k_spec = pl.BlockSpec((d, d), lambda *_: (0, 0), pipeline_mode=pl.Buffered(1))
    w_v_spec = pl.BlockSpec((d, d), lambda *_: (0, 1), pipeline_mode=pl.Buffered(1))
    return pl.pallas_call(
        _qkv_kernel,
        grid=(b, s // tm),
        in_specs=[row, _resident((1, d)), _resident((1, d)), _resident((d, d)), w_k_spec, w_v_spec],
        out_specs=[col, row, col],
        out_shape=[jax.ShapeDtypeStruct((b, d, s), BF16), jax.ShapeDtypeStruct((b, s, d), BF16),
                   jax.ShapeDtypeStruct((b, d, s), BF16)],
        scratch_shapes=[pltpu.VMEM((d, d), BF16), pltpu.VMEM((d, d), BF16)],
        compiler_params=pltpu.CompilerParams(
            dimension_semantics=("arbitrary", "arbitrary"), vmem_limit_bytes=V7X_VMEM_LIMIT),
        name="qkv_proj",
    )(x3, g_q, g_kv, w_q, w_kv, w_kv)


def _bucket_thresholds(seq):
    dist = np.arange(seq)
    max_exact = N_BUCKETS // 2
    large = max_exact + (np.log(np.maximum(dist, 1).astype(np.float32) / np.float32(max_exact))
                         / np.float32(math.log(MAX_DISTANCE / max_exact))
                         * np.float32(N_BUCKETS - max_exact)).astype(np.int32)
    bucket = np.where(dist < max_exact, dist, np.minimum(large, N_BUCKETS - 1))
    assert np.all(np.diff(bucket) >= 0)
    thr = [int(np.argmax(bucket >= b)) if np.any(bucket >= b) else seq for b in range(N_BUCKETS)]
    assert np.all(bucket[MAX_DISTANCE:] == N_BUCKETS - 1)
    return thr


def _attn_pairs(nq):
    pairs = [(qi, qi, 0) for qi in range(nq)]
    pairs += [(qi, qi - n, min(n, 2)) for qi in range(nq) for n in range(1, qi + 1)]
    return np.asarray(pairs, np.int32).T


def _attn_kernel(rb_ref, pairs_ref, qt_ref, k_ref, vt_ref, lam_ref, g_ref, o_ref,
                 bias_sc, q_sc, v_sc, s_sc, mx_sc, m_sc, l_sc, acc_sc,
                 *, t, nq, thresholds, lambda_init):
    h = pl.program_id(0)
    bi = pl.program_id(1)

    @pl.when(bi == 0)
    def _():
        row = lax.broadcasted_iota(jnp.int32, (t, t), 0)
        col = lax.broadcasted_iota(jnp.int32, (t, t), 1)
        for tile, delta in ((0, 0), (1, t)):
            dist = col - row + delta
            val = jnp.full((t, t), rb_ref[0, h], F32)
            for b in range(1, N_BUCKETS):
                val = jnp.where(dist >= thresholds[b], rb_ref[b, h], val)
            val = val * LOG2E
            if tile == 0:
                val = jnp.where(dist >= 0, val, -jnp.inf)
            bias_sc[tile] = val
        bias_sc[2] = jnp.full((t, t), rb_ref[N_BUCKETS - 1, h] * LOG2E, F32)
        zero = jnp.zeros((HEAD_DIM, nq * t), BF16)
        q_sc[0, HEAD_DIM:, :] = zero
        q_sc[1, 0:HEAD_DIM, :] = zero
        v_sc[V_HEAD_DIM:, :] = jnp.ones((ONES_ROWS, nq * t), BF16)
        l_sc[...] = jnp.zeros(l_sc.shape, F32)
        acc_sc[...] = jnp.zeros(acc_sc.shape, F32)

    q_sc[0, 0:HEAD_DIM, :] = qt_ref[0, 0:HEAD_DIM, :]
    q_sc[1, HEAD_DIM:, :] = qt_ref[0, HEAD_DIM:, :]
    v_sc[0:V_HEAD_DIM, :] = vt_ref[0]
    m_sc[...] = jnp.full(m_sc.shape, -jnp.inf, F32)

    cw = V7X_MXU_WIDTH

    def pair(i):
        if isinstance(i, int) and i < nq:
            return i, i, 0, True
        if isinstance(i, int) and i >= n_pairs:
            return None
        return pairs_ref[0, i], pairs_ref[1, i], pairs_ref[2, i], False

    def aligned(start):
        return start if isinstance(start, int) else pl.multiple_of(start, cw)

    def keys_used(diagonal, c0):
        return c0 + cw if diagonal else t

    def scores(slot, pr, mp, c0):
        qi, j, kind, diagonal = pr
        cols, nk = slice(c0, c0 + cw), keys_used(diagonal, c0)
        kblk = k_ref[0, pl.ds(aligned(j * t), nk), :]
        qblk = q_sc[mp, :, pl.ds(aligned(qi * t + c0), cw)]
        s = _dot(kblk, qblk) + bias_sc[kind, 0:nk, cols]
        s_sc[slot, mp, 0:nk, cols] = s
        mx_sc[slot, mp, :, cols] = jnp.max(s, axis=0, keepdims=True)

    def softmax_pv(slot, pr, mp, c0):
        qi, j, _, diagonal = pr
        cols, nk = slice(c0, c0 + cw), keys_used(diagonal, c0)
        vblk = v_sc[:, pl.ds(aligned(j * t), nk)]
        m_old = m_sc[qi, mp, :, cols]
        m_new = jnp.maximum(m_old, mx_sc[slot, mp, :, cols])
        alpha = jnp.exp2(m_old - m_new)
        p = jnp.exp2((s_sc[slot, mp, 0:nk, cols] - m_new).astype(BF16))
        pv = _dot(vblk, p)
        l_sc[qi, mp, :, cols] = alpha * l_sc[qi, mp, :, cols] + pv[V_HEAD_DIM:V_HEAD_DIM + 1]
        acc_sc[qi, mp, :, cols] = alpha * acc_sc[qi, mp, :, cols] + pv[0:V_HEAD_DIM]
        m_sc[qi, mp, :, cols] = m_new

    n_pairs = pairs_ref.shape[1]
    pieces = [(mp, c0) for mp in range(2) for c0 in range(0, t, cw)]

    def stage(i, parity):
        for mp, c0 in pieces:
            if pair(i + 1) is not None:
                scores(1 - parity, pair(i + 1), mp, c0)
            softmax_pv(parity, pair(i), mp, c0)

    for mp, c0 in pieces:
        scores(0, pair(0), mp, c0)
    for i in range(nq):
        stage(i, i % 2)

    def trip_body(trip, carry):
        first = nq + ATTN_STAGES_PER_TRIP * trip
        for u in range(ATTN_STAGES_PER_TRIP):
            stage(first + u, (nq + u) % 2)
        return carry

    trips = (n_pairs - 1 - nq) // ATTN_STAGES_PER_TRIP
    lax.fori_loop(0, trips, trip_body, 0)
    for i in range(nq + trips * ATTN_STAGES_PER_TRIP, n_pairs):
        stage(i, i % 2)

    lv = lam_ref[...]
    lam = (jnp.exp(jnp.sum(lv[0:1] * lv[1:2], axis=-1, keepdims=True))
           - jnp.exp(jnp.sum(lv[2:3] * lv[3:4], axis=-1, keepdims=True)) + lambda_init)
    gain = g_ref[...] * (1.0 - lambda_init)
    for qi in range(nq):
        o = acc_sc[qi, 0] * (1.0 / l_sc[qi, 0]) - acc_sc[qi, 1] * (lam / l_sc[qi, 1])
        o = o * lax.rsqrt(jnp.mean(o * o, axis=0, keepdims=True) + EPS) * gain
        o_ref[0, qi * t:(qi + 1) * t, :] = o.T.astype(BF16)


def _attention(qt, k, vt, rel_bias, lam_vecs, g_subln, lambda_init):
    b, s, _ = k.shape
    t = ATTN_TILE
    assert s % t == 0 and t >= MAX_DISTANCE
    nq = s // t
    hd = V_HEAD_DIM
    pairs = _attn_pairs(nq)
    kern = functools.partial(_attn_kernel, t=t, nq=nq, thresholds=_bucket_thresholds(s),
                             lambda_init=lambda_init)
    return pl.pallas_call(
        kern,
        grid=(N_HEADS, b),
        in_specs=[
            pl.BlockSpec(memory_space=pltpu.SMEM),
            pl.BlockSpec(memory_space=pltpu.SMEM),
            pl.BlockSpec((1, hd, s), lambda h, bi: (bi, h, 0)),
            pl.BlockSpec((1, s, hd), lambda h, bi: (bi, 0, h)),
            pl.BlockSpec((1, hd, s), lambda h, bi: (bi, h, 0)),
            _resident(lam_vecs.shape),
            _resident((hd, 1)),
        ],
        out_specs=pl.BlockSpec((1, s, hd), lambda h, bi: (bi, 0, h)),
        out_shape=jax.ShapeDtypeStruct((b, s, N_HEADS * hd), BF16),
        scratch_shapes=[
            pltpu.VMEM((3, t, t), F32),
            pltpu.VMEM((2, hd, s), BF16),
            pltpu.VMEM((hd + ONES_ROWS, s), BF16),
            pltpu.VMEM((2, 2, t, t), F32),
            pltpu.VMEM((2, 2, 1, t), F32),
            pltpu.VMEM((nq, 2, 1, t), F32),
            pltpu.VMEM((nq, 2, 1, t), F32),
            pltpu.VMEM((nq, 2, hd, t), F32),
        ],
        compiler_params=pltpu.CompilerParams(
            dimension_semantics=("arbitrary", "arbitrary"),
            vmem_limit_bytes=V7X_VMEM_LIMIT),
        name="diff_attention",
    )(rel_bias, jnp.asarray(pairs), qt, k, vt, lam_vecs, g_subln)


def kernel(x, p, g_pre_mix, g_post_mix, g_pre_ffn, g_post_ffn, g_pre_ple, g_post_ple,
           w_sc_in, w_sc_conv, w_sc_out, g_kv, w_kv, rel_bias, w_q, diff_lambda,
           g_subln, w_o, w_ffn_up, w_ffn_conv, w_ffn_down, w_ple_gate, w_ple_proj):
    b, s, d = x.shape
    depth = p.shape[0]
    n_a = depth // 2
    qk_width = N_HEADS * 2 * HEAD_DIM
    x2 = x.reshape(b * s, d)
    p2 = p.reshape(depth, b * s, p.shape[-1])
    bf = lambda w: w.astype(BF16)
    assert n_a >= 1 and w_kv.shape == (d, 2 * qk_width) and qk_width == d

    later = dict(up=w_ffn_up, down=w_ffn_down, gate=w_ple_gate, proj=w_ple_proj, o=w_o, q=w_q, kv=w_kv[None])
    for i in range(depth):
        attn = None
        if i < n_a:
            x2, casted = _mixer(x2, g_pre_mix[i][None], g_post_mix[i][None], bf(w_sc_in[i]),
                                w_sc_conv[i], bf(w_sc_out[i]), s, list(later.values()) if i == 0 else [])
            if i == 0:
                wb = dict(zip(later, casted))
        else:
            j = i - n_a
            assert j == 0, "one attention layer: K/V and Q read the same stream"
            lambda_init = 0.8 - 0.6 * math.exp(-0.3 * i)
            qt, k, vt = _qkv(x2.reshape(b, s, d), g_pre_mix[i][None], g_kv[None], wb["q"][j], wb["kv"][0])
            attn = _attention(qt, k, vt, rel_bias, diff_lambda[j], g_subln[j][:, None],
                              lambda_init).reshape(b * s, d)
        gains = jnp.stack([g_pre_ffn[i], g_post_ffn[i], g_pre_ple[i], g_post_ple[i], g_post_mix[i]])
        x2 = _ffn_ple(x2, p2, i, gains, wb["up"], w_ffn_conv, wb["down"], wb["gate"], wb["proj"], s,
                      attn=attn, w_o=None if attn is None else wb["o"][i - n_a])
    return x2.reshape(b, s, d)
```

```python
import functools
import math

import numpy as np
import jax
import jax.numpy as jnp
from jax import lax
from jax.experimental import pallas as pl
from jax.experimental.pallas import tpu as pltpu

F32 = jnp.float32
BF16 = jnp.bfloat16

EPS = 1e-6
LOG2E = math.log2(math.e)
CONV_WIDTH = 3
N_HEADS = 8
HEAD_DIM = 64
V_HEAD_DIM = 2 * HEAD_DIM
N_BUCKETS = 32
MAX_DISTANCE = 128

V7X_SUBLANES = 8
V7X_MXU_WIDTH = 256
V7X_VMEM_LIMIT = 56 * 1024 * 1024

TOKEN_TILE = 512
QKV_TILE = 1024
MIXER_TILE = 1024
SUB_TILE = 256
ATTN_TILE = 512
ATTN_STAGES_PER_TRIP = 12
HALO = V7X_SUBLANES
BF16_ROWS = 2 * V7X_SUBLANES
ONES_ROWS = BF16_ROWS


def _dot(a, b):
    return jnp.dot(a, b, preferred_element_type=F32)


def _dot_nt(a, b):
    return lax.dot_general(a, b, (((1,), (1,)), ((), ())), preferred_element_type=F32)


def _normalize(x):
    return x * lax.rsqrt(jnp.mean(x * x, axis=-1, keepdims=True) + EPS)


def _rms(x, g):
    return _normalize(x) * g


def _sigmoid(x):
    return 1.0 / (1.0 + jnp.exp2(x * -LOG2E))


def _causal_conv_cols(buf, y, wconv_ref, cols, r0):
    rows = y.shape[0]
    buf[HALO + r0:HALO + r0 + rows, cols] = y
    return (wconv_ref[0:1, cols] * buf[HALO + r0 - 2:HALO + r0 - 2 + rows, cols]
            + wconv_ref[1:2, cols] * buf[HALO + r0 - 1:HALO + r0 - 1 + rows, cols]
            + wconv_ref[2:3, cols] * y)


def _emit_staggered(chains):
    depth = max(len(chain) for chain in chains)
    for step in range(depth + len(chains) - 1):
        for lag, chain in enumerate(chains):
            if 0 <= step - lag < len(chain):
                chain[step - lag]()


def _alternate(a, b):
    out = []
    for i in range(max(len(a), len(b))):
        out += a[i:i + 1] + b[i:i + 1]
    return out


def _resident(shape, layer=None):
    if layer is None:
        return pl.BlockSpec(shape, lambda *_: (0,) * len(shape), pipeline_mode=pl.Buffered(1))
    return pl.BlockSpec((None,) + tuple(shape), lambda *_: (layer,) + (0,) * len(shape),
                        pipeline_mode=pl.Buffered(1))


def _cast_plan(w, n_steps):
    rows = w.shape[1]
    r = BF16_ROWS * pl.cdiv(pl.cdiv(rows, n_steps), BF16_ROWS)
    while rows % r:
        r += BF16_ROWS
    return r, rows // r


def _mixer_kernel(*refs, tm, tiles_per_seq, d, n_cast):
    x_ref, gpre_ref, gpost_ref, win_ref, wconv_ref, wout_ref = refs[:6]
    cast_in = refs[6:6 + n_cast]
    o_ref = refs[6 + n_cast]
    cast_out = refs[7 + n_cast:7 + 2 * n_cast]
    zbuf, mbuf = refs[7 + 2 * n_cast:]
    i = pl.program_id(0)

    @pl.when(lax.rem(i, tiles_per_seq) == 0)
    def _():
        zbuf[0:HALO, :] = jnp.zeros((HALO, d), F32)

    def cast(src, dst):
        dst[...] = src[...].astype(BF16)

    casts = [functools.partial(cast, src, dst) for src, dst in zip(cast_in, cast_out)]

    cw = V7X_MXU_WIDTH

    def chain(r0):
        rows = slice(r0, r0 + SUB_TILE)
        v = {}

        def pre():
            v["h"] = _rms(x_ref[rows, :], gpre_ref[...]).astype(BF16)

        def chunk(c):
            cols = slice(c * cw, (c + 1) * cw)
            gb = _dot(v["h"], win_ref[:, c * cw:(c + 1) * cw])
            gc = _dot(v["h"], win_ref[:, d + c * cw:d + (c + 1) * cw])
            u = _dot(v["h"], win_ref[:, 2 * d + c * cw:2 * d + (c + 1) * cw])
            conv = _causal_conv_cols(zbuf, gc * u, wconv_ref, cols, r0)
            mbuf[rows, cols] = (gb * conv).astype(BF16)

        def out_proj(k):
            v.setdefault("mix", []).append(_dot(mbuf[rows, :], wout_ref[:, k * cw:(k + 1) * cw]))

        def post():
            mix = jnp.concatenate(v.pop("mix"), axis=1)
            o_ref[rows, :] = x_ref[rows, :] + _rms(mix, gpost_ref[...])

        front = [pre] + [functools.partial(chunk, c) for c in range(d // cw)]
        return front, [functools.partial(out_proj, k) for k in range(d // cw)], post

    chains = [chain(r0) for r0 in range(0, tm, SUB_TILE)]
    _emit_staggered([front for front, _, _ in chains] + [casts])
    order = list(chains[0][1])
    for (_, _, post_prev), (_, mm, _) in zip(chains, chains[1:]):
        order += _alternate(mm, [post_prev])
    order.append(chains[-1][2])
    for piece in order:
        piece()
    zbuf[0:HALO, :] = zbuf[tm:tm + HALO, :]


def _mixer(x2, g_pre, g_post, w_in, w_conv, w_out, seq, to_cast):
    t, d = x2.shape
    tm = MIXER_TILE
    n_steps = t // tm
    plans = [_cast_plan(w, n_steps) for w in to_cast]
    kern = functools.partial(_mixer_kernel, tm=tm, tiles_per_seq=seq // tm, d=d, n_cast=len(plans))
    row = pl.BlockSpec((tm, d), lambda i: (i, 0))
    cast_specs = [pl.BlockSpec((w.shape[0], r, w.shape[2]), lambda i, nb=nb: (0, jnp.minimum(i, nb - 1), 0))
                  for w, (r, nb) in zip(to_cast, plans)]
    outs = pl.pallas_call(
        kern,
        grid=(n_steps,),
        in_specs=[row, _resident((1, d)), _resident((1, d)), _resident((d, 3 * d)),
                  _resident((CONV_WIDTH, d)), _resident((d, d))] + cast_specs,
        out_specs=[row] + cast_specs,
        out_shape=[jax.ShapeDtypeStruct((t, d), F32)] + [jax.ShapeDtypeStruct(w.shape, BF16) for w in to_cast],
        scratch_shapes=[pltpu.VMEM((tm + HALO, d), F32), pltpu.VMEM((tm, d), BF16)],
        compiler_params=pltpu.CompilerParams(
            dimension_semantics=("arbitrary",), vmem_limit_bytes=V7X_VMEM_LIMIT),
        name="sconv_mixer",
    )(x2, g_pre, g_post, w_in, w_conv, w_out, *to_cast)
    return outs[0], outs[1:]


def _ffn_ple_kernel(*refs, tm, tiles_per_seq, d, f, with_attn_out):
    if with_attn_out:
        (x_ref, p_ref, a_ref, gains_ref, wo_ref, wup_ref, wconv_ref, wdown_ref, wgate_ref,
         wproj_ref, o_ref, ybuf, abuf) = refs
    else:
        (x_ref, p_ref, gains_ref, wup_ref, wconv_ref, wdown_ref, wgate_ref,
         wproj_ref, o_ref, ybuf, abuf) = refs
    i = pl.program_id(0)

    @pl.when(lax.rem(i, tiles_per_seq) == 0)
    def _():
        ybuf[0:HALO, :] = jnp.zeros((HALO, 2 * f), F32)

    cw = V7X_MXU_WIDTH

    def chain(r0):
        rows = slice(r0, r0 + SUB_TILE)
        v = {}

        def pre():
            x = x_ref[rows, :]
            if with_attn_out:
                x = x + _rms(jnp.concatenate(v.pop("mix"), axis=1), gains_ref[4:5, :])
            v["x"] = x
            v["hn"] = _rms(x, gains_ref[0:1, :]).astype(BF16)

        def attn_out(k):
            v.setdefault("mix", []).append(_dot(a_ref[rows, :], wo_ref[:, k * cw:(k + 1) * cw]))

        def chunk(c):
            gcols = slice(c * cw, (c + 1) * cw)
            ucols = slice(f + c * cw, f + (c + 1) * cw)
            gate = _causal_conv_cols(ybuf, _dot(v["hn"], wup_ref[:, c * cw:(c + 1) * cw]),
                                     wconv_ref, gcols, r0)
            up = _causal_conv_cols(ybuf, _dot(v["hn"], wup_ref[:, f + c * cw:f + (c + 1) * cw]),
                                   wconv_ref, ucols, r0)
            abuf[rows, gcols] = (gate * _sigmoid(gate) * up).astype(BF16)

        def down(k):
            v.setdefault("ffn", []).append(_dot(abuf[rows, :], wdown_ref[:, k * cw:(k + 1) * cw]))

        def ffn_norm():
            v["x"] = v["x"] + _rms(jnp.concatenate(v.pop("ffn"), axis=1), gains_ref[1:2, :])

        def embed_norm():
            v["hg"] = _rms(v["x"], gains_ref[2:3, :]).astype(BF16)

        def gate(k):
            v.setdefault("gate", []).append(_dot(v["hg"], wgate_ref[:, k * cw:(k + 1) * cw]))

        def proj():
            v["emb"] = _dot(p_ref[rows, :].astype(BF16), wproj_ref[...])

        def gated():
            v["ple"] = _sigmoid(jnp.concatenate(v.pop("gate"), axis=1)) * v.pop("emb")

        def store():
            o_ref[rows, :] = v["x"] + _rms(v.pop("ple"), gains_ref[3:4, :])

        nk = d // cw
        front = [pre] + [functools.partial(chunk, c) for c in range(f // cw)]
        matmuls = ([functools.partial(down, k) for k in range(nk)]
                   + [functools.partial(gate, k) for k in range(nk)] + [proj])
        elementwise = [ffn_norm, embed_norm, gated, store]
        opening = [functools.partial(attn_out, k) for k in range(nk)] if with_attn_out else []
        return opening, front, matmuls, elementwise, nk

    (open_a, front_a, mm_a, ew_a, nk), (open_b, front_b, mm_b, ew_b, _) = [
        chain(r0) for r0 in range(0, tm, SUB_TILE)]
    order = open_a + _alternate(open_b, front_a[:1]) + _alternate(front_a[1:], front_b)
    order += mm_a[:nk]
    order += _alternate(mm_b[:nk], ew_a[:2])
    order += _alternate(mm_a[nk:], ew_b[:2])
    order += _alternate(mm_b[nk:], ew_a[2:])
    order += ew_b[2:]
    for piece in order:
        piece()
    ybuf[0:HALO, :] = ybuf[tm:tm + HALO, :]


def _ffn_ple(x2, p3, layer, gains, w_up, w_conv, w_down, w_gate, w_proj, seq, attn=None, w_o=None):
    t, d = x2.shape
    f = w_down.shape[1]
    pd = p3.shape[2]
    tm = TOKEN_TILE
    with_attn_out = attn is not None
    kern = functools.partial(_ffn_ple_kernel, tm=tm, tiles_per_seq=seq // tm, d=d, f=f,
                             with_attn_out=with_attn_out)
    row = pl.BlockSpec((tm, d), lambda i: (i, 0))
    prow = pl.BlockSpec((None, tm, pd), lambda i: (layer, i, 0))
    args = [x2, p3]
    specs = [row, prow]
    if with_attn_out:
        args.append(attn)
        specs.append(row)
    args.append(gains)
    specs.append(_resident(gains.shape))
    if with_attn_out:
        args.append(w_o)
        specs.append(_resident((d, d)))
    args += [w_up, w_conv, w_down, w_gate, w_proj]
    specs += [_resident((d, 2 * f), layer), _resident((CONV_WIDTH, 2 * f), layer), _resident((f, d), layer),
              _resident((d, d), layer), _resident((pd, d), layer)]
    return pl.pallas_call(
        kern,
        grid=(t // tm,),
        in_specs=specs,
        out_specs=row,
        out_shape=jax.ShapeDtypeStruct((t, d), F32),
        scratch_shapes=[pltpu.VMEM((tm + HALO, 2 * f), F32), pltpu.VMEM((tm, f), BF16)],
        compiler_params=pltpu.CompilerParams(
            dimension_semantics=("arbitrary",), vmem_limit_bytes=V7X_VMEM_LIMIT),
        name="attn_out_ffn_ple" if with_attn_out else "ffn_ple",
    )(*args)


def _qkv_kernel(x_ref, gq_ref, gkv_ref, wq_ref, wk_ref, wv_ref, qt_ref, k_ref, vt_ref, wqt_ref, wvt_ref):
    @pl.when((pl.program_id(0) == 0) & (pl.program_id(1) == 0))
    def _():
        wqt_ref[...] = wq_ref[...].T
        wvt_ref[...] = wv_ref[...].T

    def chain(r0):
        rows = slice(r0, r0 + SUB_TILE)
        v = {}

        def pre():
            xn = _normalize(x_ref[0, rows, :])
            v["h"] = (xn * gq_ref[...]).astype(BF16)
            v["hk"] = (xn * gkv_ref[...]).astype(BF16)

        def q_proj():
            qt_ref[0, :, rows] = (_dot_nt(wqt_ref[...], v["h"]) * (HEAD_DIM ** -0.5 * LOG2E)).astype(BF16)

        def k_proj():
            k_ref[0, rows, :] = _dot(v["hk"], wk_ref[...]).astype(BF16)

        def v_proj():
            vt_ref[0, :, rows] = _dot_nt(wvt_ref[...], v["hk"]).astype(BF16)

        return [pre, q_proj, k_proj, v_proj]

    _emit_staggered([chain(r0) for r0 in range(0, x_ref.shape[1], SUB_TILE)])


def _qkv(x3, g_q, g_kv, w_q, w_kv):
    b, s, d = x3.shape
    tm = QKV_TILE
    row = pl.BlockSpec((1, tm, d), lambda bi, i: (bi, i, 0))
    col = pl.BlockSpec((1, d, tm), lambda bi, i: (bi, 0, i))
    w_k_spec = pl.BlockSpec((d, d), lambda *_: (0, 0), pipeline_mode=pl.Buffered(1))
    w_v_spec = pl.BlockSpec((d, d), lambda *_: (0, 1), pipeline_mode=pl.Buffered(1))
    return pl.pallas_call(
        _qkv_kernel,
        grid=(b, s // tm),
        in_specs=[row, _resident((1, d)), _resident((1, d)), _resident((d, d)), w_k_spec, w_v_spec],
        out_specs=[col, row, col],
        out_shape=[jax.ShapeDtypeStruct((b, d, s), BF16), jax.ShapeDtypeStruct((b, s, d), BF16),
                   jax.ShapeDtypeStruct((b, d, s), BF16)],
        scratch_shapes=[pltpu.VMEM((d, d), BF16), pltpu.VMEM((d, d), BF16)],
        compiler_params=pltpu.CompilerParams(
            dimension_semantics=("arbitrary", "arbitrary"), vmem_limit_bytes=V7X_VMEM_LIMIT),
        name="qkv_proj",
    )(x3, g_q, g_kv, w_q, w_kv, w_kv)


def _bucket_thresholds(seq):
    dist = np.arange(seq)
    max_exact = N_BUCKETS // 2
    large = max_exact + (np.log(np.maximum(dist, 1).astype(np.float32) / np.float32(max_exact))
                         / np.float32(math.log(MAX_DISTANCE / max_exact))
                         * np.float32(N_BUCKETS - max_exact)).astype(np.int32)
    bucket = np.where(dist < max_exact, dist, np.minimum(large, N_BUCKETS - 1))
    assert np.all(np.diff(bucket) >= 0)
    thr = [int(np.argmax(bucket >= b)) if np.any(bucket >= b) else seq for b in range(N_BUCKETS)]
    assert np.all(bucket[MAX_DISTANCE:] == N_BUCKETS - 1)
    return thr


def _attn_pairs(nq):
    pairs = [(qi, qi, 0) for qi in range(nq)]
    pairs += [(qi, qi - n, min(n, 2)) for qi in range(nq) for n in range(1, qi + 1)]
    return np.asarray(pairs, np.int32).T


def _attn_kernel(rb_ref, pairs_ref, qt_ref, k_ref, vt_ref, lam_ref, g_ref, o_ref,
                 bias_sc, q_sc, v_sc, s_sc, mx_sc, m_sc, l_sc, acc_sc,
                 *, t, nq, thresholds, lambda_init):
    h = pl.program_id(0)
    bi = pl.program_id(1)

    @pl.when(bi == 0)
    def _():
        row = lax.broadcasted_iota(jnp.int32, (t, t), 0)
        col = lax.broadcasted_iota(jnp.int32, (t, t), 1)
        for tile, delta in ((0, 0), (1, t)):
            dist = col - row + delta
            val = jnp.full((t, t), rb_ref[0, h], F32)
            for b in range(1, N_BUCKETS):
                val = jnp.where(dist >= thresholds[b], rb_ref[b, h], val)
            val = val * LOG2E
            if tile == 0:
                val = jnp.where(dist >= 0, val, -jnp.inf)
            bias_sc[tile] = val
        bias_sc[2] = jnp.full((t, t), rb_ref[N_BUCKETS - 1, h] * LOG2E, F32)
        zero = jnp.zeros((HEAD_DIM, nq * t), BF16)
        q_sc[0, HEAD_DIM:, :] = zero
        q_sc[1, 0:HEAD_DIM, :] = zero
        v_sc[V_HEAD_DIM:, :] = jnp.ones((ONES_ROWS, nq * t), BF16)
        l_sc[...] = jnp.zeros(l_sc.shape, F32)
        acc_sc[...] = jnp.zeros(acc_sc.shape, F32)

    q_sc[0, 0:HEAD_DIM, :] = qt_ref[0, 0:HEAD_DIM, :]
    q_sc[1, HEAD_DIM:, :] = qt_ref[0, HEAD_DIM:, :]
    v_sc[0:V_HEAD_DIM, :] = vt_ref[0]
    m_sc[...] = jnp.full(m_sc.shape, -jnp.inf, F32)

    cw = V7X_MXU_WIDTH

    def pair(i):
        if isinstance(i, int) and i < nq:
            return i, i, 0, True
        if isinstance(i, int) and i >= n_pairs:
            return None
        return pairs_ref[0, i], pairs_ref[1, i], pairs_ref[2, i], False

    def aligned(start):
        return start if isinstance(start, int) else pl.multiple_of(start, cw)

    def keys_used(diagonal, c0):
        return c0 + cw if diagonal else t

    def scores(slot, pr, mp, c0):
        qi, j, kind, diagonal = pr
        cols, nk = slice(c0, c0 + cw), keys_used(diagonal, c0)
        kblk = k_ref[0, pl.ds(aligned(j * t), nk), :]
        qblk = q_sc[mp, :, pl.ds(aligned(qi * t + c0), cw)]
        s = _dot(kblk, qblk) + bias_sc[kind, 0:nk, cols]
        s_sc[slot, mp, 0:nk, cols] = s
        mx_sc[slot, mp, :, cols] = jnp.max(s, axis=0, keepdims=True)

    def softmax_pv(slot, pr, mp, c0):
        qi, j, _, diagonal = pr
        cols, nk = slice(c0, c0 + cw), keys_used(diagonal, c0)
        vblk = v_sc[:, pl.ds(aligned(j * t), nk)]
        m_old = m_sc[qi, mp, :, cols]
        m_new = jnp.maximum(m_old, mx_sc[slot, mp, :, cols])
        alpha = jnp.exp2(m_old - m_new)
        p = jnp.exp2((s_sc[slot, mp, 0:nk, cols] - m_new).astype(BF16))
        pv = _dot(vblk, p)
        l_sc[qi, mp, :, cols] = alpha * l_sc[qi, mp, :, cols] + pv[V_HEAD_DIM:V_HEAD_DIM + 1]
        acc_sc[qi, mp, :, cols] = alpha * acc_sc[qi, mp, :, cols] + pv[0:V_HEAD_DIM]
        m_sc[qi, mp, :, cols] = m_new

    n_pairs = pairs_ref.shape[1]
    pieces = [(mp, c0) for mp in range(2) for c0 in range(0, t, cw)]

    def stage(i, parity):
        for mp, c0 in pieces:
            if pair(i + 1) is not None:
                scores(1 - parity, pair(i + 1), mp, c0)
            softmax_pv(parity, pair(i), mp, c0)

    for mp, c0 in pieces:
        scores(0, pair(0), mp, c0)
    for i in range(nq):
        stage(i, i % 2)

    def trip_body(trip, carry):
        first = nq + ATTN_STAGES_PER_TRIP * trip
        for u in range(ATTN_STAGES_PER_TRIP):
            stage(first + u, (nq + u) % 2)
        return carry

    trips = (n_pairs - 1 - nq) // ATTN_STAGES_PER_TRIP
    lax.fori_loop(0, trips, trip_body, 0)
    for i in range(nq + trips * ATTN_STAGES_PER_TRIP, n_pairs):
        stage(i, i % 2)

    lv = lam_ref[...]
    lam = (jnp.exp(jnp.sum(lv[0:1] * lv[1:2], axis=-1, keepdims=True))
           - jnp.exp(jnp.sum(lv[2:3] * lv[3:4], axis=-1, keepdims=True)) + lambda_init)
    gain = g_ref[...] * (1.0 - lambda_init)
    for qi in range(nq):
        o = acc_sc[qi, 0] * (1.0 / l_sc[qi, 0]) - acc_sc[qi, 1] * (lam / l_sc[qi, 1])
        o = o * lax.rsqrt(jnp.mean(o * o, axis=0, keepdims=True) + EPS) * gain
        o_ref[0, qi * t:(qi + 1) * t, :] = o.T.astype(BF16)


def _attention(qt, k, vt, rel_bias, lam_vecs, g_subln, lambda_init):
    b, s, _ = k.shape
    t = ATTN_TILE
    assert s % t == 0 and t >= MAX_DISTANCE
    nq = s // t
    hd = V_HEAD_DIM
    pairs = _attn_pairs(nq)
    kern = functools.partial(_attn_kernel, t=t, nq=nq, thresholds=_bucket_thresholds(s),
                             lambda_init=lambda_init)
    return pl.pallas_call(
        kern,
        grid=(N_HEADS, b),
        in_specs=[
            pl.BlockSpec(memory_space=pltpu.SMEM),
            pl.BlockSpec(memory_space=pltpu.SMEM),
            pl.BlockSpec((1, hd, s), lambda h, bi: (bi, h, 0)),
            pl.BlockSpec((1, s, hd), lambda h, bi: (bi, 0, h)),
            pl.BlockSpec((1, hd, s), lambda h, bi: (bi, h, 0)),
            _resident(lam_vecs.shape),
            _resident((hd, 1)),
        ],
        out_specs=pl.BlockSpec((1, s, hd), lambda h, bi: (bi, 0, h)),
        out_shape=jax.ShapeDtypeStruct((b, s, N_HEADS * hd), BF16),
        scratch_shapes=[
            pltpu.VMEM((3, t, t), F32),
            pltpu.VMEM((2, hd, s), BF16),
            pltpu.VMEM((hd + ONES_ROWS, s), BF16),
            pltpu.VMEM((2, 2, t, t), F32),
            pltpu.VMEM((2, 2, 1, t), F32),
            pltpu.VMEM((nq, 2, 1, t), F32),
            pltpu.VMEM((nq, 2, 1, t), F32),
            pltpu.VMEM((nq, 2, hd, t), F32),
        ],
        compiler_params=pltpu.CompilerParams(
            dimension_semantics=("arbitrary", "arbitrary"),
            vmem_limit_bytes=V7X_VMEM_LIMIT),
        name="diff_attention",
    )(rel_bias, jnp.asarray(pairs), qt, k, vt, lam_vecs, g_subln)


def kernel(x, p, g_pre_mix, g_post_mix, g_pre_ffn, g_post_ffn, g_pre_ple, g_post_ple,
           w_sc_in, w_sc_conv, w_sc_out, g_kv, w_kv, rel_bias, w_q, diff_lambda,
           g_subln, w_o, w_ffn_up, w_ffn_conv, w_ffn_down, w_ple_gate, w_ple_proj):
    b, s, d = x.shape
    depth = p.shape[0]
    n_a = depth // 2
    qk_width = N_HEADS * 2 * HEAD_DIM
    x2 = x.reshape(b * s, d)
    p2 = p.reshape(depth, b * s, p.shape[-1])
    bf = lambda w: w.astype(BF16)
    assert n_a >= 1 and w_kv.shape == (d, 2 * qk_width) and qk_width == d

    later = dict(up=w_ffn_up, down=w_ffn_down, gate=w_ple_gate, proj=w_ple_proj, o=w_o, q=w_q, kv=w_kv[None])
    for i in range(depth):
        attn = None
        if i < n_a:
            x2, casted = _mixer(x2, g_pre_mix[i][None], g_post_mix[i][None], bf(w_sc_in[i]),
                                w_sc_conv[i], bf(w_sc_out[i]), s, list(later.values()) if i == 0 else [])
            if i == 0:
                wb = dict(zip(later, casted))
        else:
            j = i - n_a
            assert j == 0, "one attention layer: K/V and Q read the same stream"
            lambda_init = 0.8 - 0.6 * math.exp(-0.3 * i)
            qt, k, vt = _qkv(x2.reshape(b, s, d), g_pre_mix[i][None], g_kv[None], wb["q"][j], wb["kv"][0])
            attn = _attention(qt, k, vt, rel_bias, diff_lambda[j], g_subln[j][:, None],
                              lambda_init).reshape(b * s, d)
        gains = jnp.stack([g_pre_ffn[i], g_post_ffn[i], g_pre_ple[i], g_post_ple[i], g_post_mix[i]])
        x2 = _ffn_ple(x2, p2, i, gains, wb["up"], w_ffn_conv, wb["down"], wb["gate"], wb["proj"], s,
                      attn=attn, w_o=None if attn is None else wb["o"][i - n_a])
    return x2.reshape(b, s, d)
```
